```python
import jax, jax.numpy as jnp
from jax import lax
import numpy as np

D_MODEL = 1024
BATCH = 8
SEQ = 4096
DEPTH = 2

N_MIXERS = 2
N_GLA = (DEPTH + 1) // 2
N_NSA = DEPTH // 2
NORM_EPS = 1e-6
ROPE_THETA = 10000.0
NEG = -1e30
BIG = 1e9

GLA_HEADS = 4
GLA_DK = D_MODEL // 2 // GLA_HEADS
GLA_DV = D_MODEL // GLA_HEADS
GLA_QK = GLA_HEADS * GLA_DK
GLA_V = GLA_HEADS * GLA_DV
GLA_GATE_RANK = 16
GLA_GATE_NORMALIZER = 16.0
GLA_CHUNK = 64
GLA_IN = 2 * GLA_QK + GLA_V + GLA_GATE_RANK + GLA_V

NSA_HEADS = 16
NSA_GROUPS = 4
NSA_HPG = NSA_HEADS // NSA_GROUPS
NSA_DH = D_MODEL // NSA_HEADS
NSA_Q = NSA_HEADS * NSA_DH
NSA_KV = NSA_GROUPS * NSA_DH
CMP_BLOCK = 32
CMP_STRIDE = 16
CMP_HIDDEN = 256
SEL_BLOCK = 64
SEL_TOP = 16
WINDOW = 512
Q_BLOCK = 64
NSA_IN = NSA_Q + 6 * NSA_KV + 3 * NSA_HEADS + NSA_Q

kernel_name = "hybrid_gla_nsa_sandwich"


def rms_norm(x, g):
    xf = x.astype(jnp.float32)
    y = xf * lax.rsqrt(jnp.mean(xf * xf, axis=-1, keepdims=True) + NORM_EPS)
    return (y * g.astype(jnp.float32)).astype(x.dtype)


def rope(x, positions):
    dh = x.shape[-1]
    inv = ROPE_THETA ** (-jnp.arange(0, dh, 2, dtype=jnp.float32) / dh)
    ang = positions.astype(jnp.float32)[..., None] * inv
    cos = jnp.cos(ang)[:, :, None, :]
    sin = jnp.sin(ang)[:, :, None, :]
    x1, x2 = jnp.split(x, 2, axis=-1)
    return jnp.concatenate([x1 * cos - x2 * sin, x2 * cos + x1 * sin], axis=-1)


def gla_chunked(q, k, v, g):
    B, H, S, dk = q.shape
    dv = v.shape[-1]
    C = GLA_CHUNK
    N = S // C
    q = q.reshape(B, H, N, C, dk)
    k = k.reshape(B, H, N, C, dk)
    v = v.reshape(B, H, N, C, dv)
    b = jnp.cumsum(g.reshape(B, H, N, C, dk), axis=3)
    b_last = b[:, :, :, -1:]
    q_e = q * jnp.exp(b)
    k_e = k * jnp.exp(-b)
    k_d = k * jnp.exp(b_last - b)
    causal = jnp.tril(jnp.ones((C, C), dtype=bool))
    a = jnp.where(causal, jnp.einsum('bhnid,bhnjd->bhnij', q_e, k_e), 0.0)
    o_intra = jnp.einsum('bhnij,bhnjv->bhniv', a, v)

    def step(state, xs):
        qe, kd, vc, dec = xs
        o = jnp.einsum('bhcd,bhdv->bhcv', qe, state)
        state = dec[..., None] * state + jnp.einsum('bhcd,bhcv->bhdv', kd, vc)
        return state, o

    xs = (jnp.moveaxis(q_e, 2, 0), jnp.moveaxis(k_d, 2, 0), jnp.moveaxis(v, 2, 0),
          jnp.moveaxis(jnp.exp(b_last[:, :, :, 0]), 2, 0))
    _, o_inter = lax.scan(step, jnp.zeros((B, H, dk, dv), jnp.float32), xs)
    return (o_intra + jnp.moveaxis(o_inter, 0, 2)).reshape(B, H, S, dv)


def gla_mixer(h, w_in, w_gk_up, b_gk, head_norm, w_out):
    B, S, _ = h.shape
    p = (h @ w_in).astype(jnp.float32)
    q, k, v, glr, z = jnp.split(p, [GLA_QK, 2 * GLA_QK, 2 * GLA_QK + GLA_V,
                                    2 * GLA_QK + GLA_V + GLA_GATE_RANK], axis=-1)
    gk = jax.nn.log_sigmoid(glr @ w_gk_up.astype(jnp.float32) + b_gk.astype(jnp.float32)) / GLA_GATE_NORMALIZER

    def heads(t, d):
        return t.reshape(B, S, GLA_HEADS, d).transpose(0, 2, 1, 3)

    o = gla_chunked(heads(q, GLA_DK) * GLA_DK ** -0.5, heads(k, GLA_DK),
                    heads(v, GLA_DV), heads(gk, GLA_DK))
    o = rms_norm(o, head_norm)
    o = o.transpose(0, 2, 1, 3).reshape(B, S, GLA_V) * jax.nn.silu(z)
    return o @ w_out.astype(jnp.float32)


def compress(t, pe, w1, w2):
    B, S, G, DH = t.shape
    r = CMP_BLOCK // CMP_STRIDE
    nc = (S - CMP_BLOCK) // CMP_STRIDE + 1
    pieces = t.reshape(B, S // CMP_STRIDE, CMP_STRIDE, G, DH)
    blocks = jnp.concatenate([pieces[:, i:i + nc] for i in range(r)], axis=2)
    blocks = blocks + pe.astype(jnp.float32)[None, None, :, None, :]
    flat = blocks.transpose(0, 3, 1, 2, 4).reshape(B, G, nc, CMP_BLOCK * DH)
    return jax.nn.silu(flat @ w1.astype(jnp.float32)) @ w2.astype(jnp.float32)


def nsa_mixer(h, positions, w_in, b_gate, pe_k, pe_v, ck_w1, ck_w2, cv_w1, cv_w2, w_out):
    B, S, _ = h.shape
    G, HPG, DH = NSA_GROUPS, NSA_HPG, NSA_DH
    p = (h @ w_in).astype(jnp.float32)
    splits = [NSA_Q + i * NSA_KV for i in range(7)] + [NSA_Q + 6 * NSA_KV + 3 * NSA_HEADS]
    q, kc, vc, ks, vs, kw, vw, gl, z = jnp.split(p, splits, axis=-1)

    q = rope(q.reshape(B, S, NSA_HEADS, DH), positions) * DH ** -0.5
    kc = rope(kc.reshape(B, S, G, DH), positions)
    ks = rope(ks.reshape(B, S, G, DH), positions)
    kw = rope(kw.reshape(B, S, G, DH), positions)
    vc = vc.reshape(B, S, G, DH)
    vs = vs.reshape(B, S, G, DH)
    vw = vw.reshape(B, S, G, DH)

    kcmp = compress(kc, pe_k, ck_w1, ck_w2)
    vcmp = compress(vc, pe_v, cv_w1, cv_w2)
    nc = kcmp.shape[2]
    ns = S // SEL_BLOCK
    nq = S // Q_BLOCK
    n_top = min(SEL_TOP, ns)

    ksb = ks.reshape(B, ns, SEL_BLOCK, G, DH).transpose(0, 3, 1, 2, 4)
    vsb = vs.reshape(B, ns, SEL_BLOCK, G, DH).transpose(0, 3, 1, 2, 4)
    pad = ((0, 0), (0, 0), (WINDOW, 0), (0, 0))
    kwp = jnp.pad(kw.transpose(0, 2, 1, 3), pad)
    vwp = jnp.pad(vw.transpose(0, 2, 1, 3), pad)

    cmp_start = jnp.arange(nc) * CMP_STRIDE
    cmp_end = cmp_start + CMP_BLOCK - 1
    sel_start = jnp.arange(ns) * SEL_BLOCK
    overlap = ((cmp_start[:, None] < sel_start[None, :] + SEL_BLOCK) &
               (cmp_start[:, None] + CMP_BLOCK > sel_start[None, :])).astype(jnp.float32)
    jb = jnp.arange(ns)
    bidx = jnp.arange(B)[:, None, None, None]
    gidx = jnp.arange(G)[None, :, None, None]

    qblocks = q.reshape(B, nq, Q_BLOCK, G, HPG, DH).transpose(1, 0, 3, 4, 2, 5)

    def block(args):
        n, qb = args
        t = n * Q_BLOCK + jnp.arange(Q_BLOCK)
        s = jnp.einsum('bghtd,bgcd->bghtc', qb, kcmp)
        valid = cmp_end[None, :] <= t[:, None]
        pc = jax.nn.softmax(jnp.where(valid, s, NEG), axis=-1)
        pc = jnp.where(jnp.any(valid, axis=-1)[:, None], pc, 0.0)
        o_c = jnp.einsum('bghtc,bgcd->bghtd', pc, vcmp)
        imp = jnp.einsum('bghtc,cj->bgtj', pc, overlap)
        cur = (t // SEL_BLOCK)[:, None]
        forced = (jb[None] == 0) | (jb[None] == cur) | (jb[None] == cur - 1)
        causal_blk = sel_start[None, :] <= t[:, None]
        imp = jnp.where(forced, BIG, jnp.where(causal_blk, imp, NEG))
        _, idx = lax.top_k(imp, n_top)
        k_sel = ksb[bidx, gidx, idx]
        v_sel = vsb[bidx, gidx, idx]
        pos = idx[..., None] * SEL_BLOCK + jnp.arange(SEL_BLOCK)
        smask = (pos <= t[:, None, None])[:, :, None]
        s = jnp.where(smask, jnp.einsum('bghtd,bgtksd->bghtks', qb, k_sel), NEG)
        ps = jax.nn.softmax(s.reshape(B, G, HPG, Q_BLOCK, n_top * SEL_BLOCK), axis=-1).reshape(s.shape)
        o_s = jnp.einsum('bghtks,bgtksd->bghtd', ps, v_sel)
        kwin = lax.dynamic_slice_in_dim(kwp, n * Q_BLOCK, WINDOW + Q_BLOCK, axis=2)
        vwin = lax.dynamic_slice_in_dim(vwp, n * Q_BLOCK, WINDOW + Q_BLOCK, axis=2)
        wpos = n * Q_BLOCK - WINDOW + jnp.arange(WINDOW + Q_BLOCK)
        dlt = t[:, None] - wpos[None, :]
        wmask = (dlt >= 0) & (dlt < WINDOW) & (wpos[None, :] >= 0)
        s = jnp.where(wmask, jnp.einsum('bghtd,bgsd->bghts', qb, kwin), NEG)
        o_w = jnp.einsum('bghts,bgsd->bghtd', jax.nn.softmax(s, axis=-1), vwin)
        return jnp.stack([o_c, o_s, o_w], axis=0)

    out = lax.map(block, (jnp.arange(nq), qblocks))
    out = out.transpose(2, 0, 5, 3, 4, 1, 6).reshape(B, S, NSA_HEADS, 3, DH)
    gates = jax.nn.sigmoid(gl + b_gate.astype(jnp.float32)).reshape(B, S, NSA_HEADS, 3)
    o = jnp.einsum('bshc,bshcd->bshd', gates, out).reshape(B, S, NSA_Q) * jax.nn.silu(z)
    return o @ w_out.astype(jnp.float32)


def setup_inputs(seed: int = 0) -> dict:
    key = jax.random.key(seed)
    ks = jax.random.split(key, 20)
    nrm = jax.random.normal
    f32 = jnp.float32
    return {
        "x": nrm(ks[0], (BATCH, SEQ, D_MODEL), f32),
        "positions": jnp.broadcast_to(jnp.arange(SEQ, dtype=jnp.int32), (BATCH, SEQ)),
        "pre_norm": 1.0 + 0.05 * nrm(ks[1], (DEPTH, D_MODEL), f32),
        "post_norm": 1.0 + 0.05 * nrm(ks[2], (DEPTH, D_MODEL), f32),
        "gla_w_in": nrm(ks[3], (N_GLA, D_MODEL, GLA_IN), f32) * D_MODEL ** -0.5,
        "gla_w_gk_up": nrm(ks[4], (N_GLA, GLA_GATE_RANK, GLA_QK), f32) * GLA_GATE_RANK ** -0.5,
        "gla_b_gk": 0.02 * nrm(ks[5], (N_GLA, GLA_QK), f32),
        "gla_head_norm": 1.0 + 0.05 * nrm(ks[6], (N_GLA, GLA_DV), f32),
        "gla_w_out": nrm(ks[7], (N_GLA, GLA_V, D_MODEL), f32) * GLA_V ** -0.5,
        "nsa_w_in": nrm(ks[8], (N_NSA, D_MODEL, NSA_IN), f32) * D_MODEL ** -0.5,
        "nsa_b_gate": 0.02 * nrm(ks[9], (N_NSA, 3 * NSA_HEADS), f32),
        "nsa_pe_k": 0.02 * nrm(ks[10], (N_NSA, CMP_BLOCK, NSA_DH), f32),
        "nsa_pe_v": 0.02 * nrm(ks[11], (N_NSA, CMP_BLOCK, NSA_DH), f32),
        "nsa_ck_w1": nrm(ks[12], (N_NSA, CMP_BLOCK * NSA_DH, CMP_HIDDEN), f32) * (CMP_BLOCK * NSA_DH) ** -0.5,
        "nsa_ck_w2": nrm(ks[13], (N_NSA, CMP_HIDDEN, NSA_DH), f32) * CMP_HIDDEN ** -0.5,
        "nsa_cv_w1": nrm(ks[14], (N_NSA, CMP_BLOCK * NSA_DH, CMP_HIDDEN), f32) * (CMP_BLOCK * NSA_DH) ** -0.5,
        "nsa_cv_w2": nrm(ks[15], (N_NSA, CMP_HIDDEN, NSA_DH), f32) * CMP_HIDDEN ** -0.5,
        "nsa_w_out": nrm(ks[16], (N_NSA, NSA_Q, D_MODEL), f32) * NSA_Q ** -0.5,
    }


def reference(x, positions, pre_norm, post_norm, gla_w_in, gla_w_gk_up, gla_b_gk, gla_head_norm,
              gla_w_out, nsa_w_in, nsa_b_gate, nsa_pe_k, nsa_pe_v, nsa_ck_w1, nsa_ck_w2,
              nsa_cv_w1, nsa_cv_w2, nsa_w_out):
    for i in range(DEPTH):
        h = rms_norm(x, pre_norm[i])
        j = i // N_MIXERS
        if i % N_MIXERS == 0:
            y = gla_mixer(h, gla_w_in[j], gla_w_gk_up[j], gla_b_gk[j], gla_head_norm[j], gla_w_out[j])
        else:
            y = nsa_mixer(h, positions, nsa_w_in[j], nsa_b_gate[j], nsa_pe_k[j], nsa_pe_v[j],
                          nsa_ck_w1[j], nsa_ck_w2[j], nsa_cv_w1[j], nsa_cv_w2[j], nsa_w_out[j])
        x = x + rms_norm(y.astype(x.dtype), post_norm[i])
    return x
```

```python
import functools

import jax
import jax.numpy as jnp
from jax import lax
from jax.experimental import pallas as pl
from jax.experimental.pallas import tpu as pltpu

F32 = jnp.float32
BF16 = jnp.bfloat16

D_MODEL = 1024
NORM_EPS = 1e-6
ROPE_THETA = 10000.0
NEG = -1e30
BIG = 1e9

GLA_HEADS = 4
GLA_DK = 128
GLA_DV = 256
GLA_QK = GLA_HEADS * GLA_DK
GLA_V = GLA_HEADS * GLA_DV
GLA_GATE_RANK = 16
GLA_GATE_NORMALIZER = 16.0
GLA_CHUNK = 64

NSA_HEADS = 16
NSA_GROUPS = 4
NSA_HPG = NSA_HEADS // NSA_GROUPS
NSA_DH = 64
NSA_Q = NSA_HEADS * NSA_DH
NSA_KV = NSA_GROUPS * NSA_DH
CMP_BLOCK = 32
CMP_STRIDE = 16
CMP_HIDDEN = 256
SEL_BLOCK = 64
SEL_TOP = 16
WINDOW = 512

LANES = 128
ROW_TILE = 512
GLA_SEQ_TILE = 256
ATT_TILE = 256
VMEM_LIMIT = 56 * 1024 * 1024


def _dot(a, b):
    return jnp.dot(a, b, preferred_element_type=F32)


def _dot_nt(a, b):
    return lax.dot_general(a, b, (((1,), (1,)), ((), ())), preferred_element_type=F32)


def _rms(x, g):
    return x * lax.rsqrt(jnp.mean(x * x, axis=-1, keepdims=True) + NORM_EPS) * g


def _silu(z):
    return z * jax.nn.sigmoid(z)


def _params(*sem):
    return pltpu.CompilerParams(dimension_semantics=sem, vmem_limit_bytes=VMEM_LIMIT)


def _gla_proj_kernel(x_ref, g_ref, wm_ref, wg_ref, wup_ref, bgk_ref,
                     q_ref, k_ref, v_ref, gk_ref, zs_ref):
    h = _rms(x_ref[...], g_ref[...]).astype(BF16)
    q_ref[...] = _dot(h, wm_ref[:, 0:GLA_QK])
    k_ref[...] = _dot(h, wm_ref[:, GLA_QK:2 * GLA_QK])
    v_ref[...] = _dot(h, wm_ref[:, 2 * GLA_QK:2 * GLA_QK + GLA_V]).astype(BF16)
    zs_ref[...] = _silu(_dot(h, wm_ref[:, 2 * GLA_QK + GLA_V:])).astype(BF16)
    glr = _dot(h, wg_ref[...])
    pre = _dot(glr.astype(BF16), wup_ref[...]) + bgk_ref[...]
    gk_ref[...] = (jnp.minimum(pre, 0.0) - jnp.log(1.0 + jnp.exp(-jnp.abs(pre)))) / GLA_GATE_NORMALIZER


def _gla_proj(x2, pre_g, w_in, w_gk_up, b_gk):
    T = x2.shape[0]
    o_gl = 2 * GLA_QK + GLA_V
    wm = jnp.concatenate([w_in[:, :o_gl], w_in[:, o_gl + GLA_GATE_RANK:]], axis=1).astype(BF16)
    wg = jnp.pad(w_in[:, o_gl:o_gl + GLA_GATE_RANK], ((0, 0), (0, LANES - GLA_GATE_RANK))).astype(BF16)
    wup = jnp.pad(w_gk_up, ((0, LANES - GLA_GATE_RANK), (0, 0))).astype(BF16)
    row = lambda w: pl.BlockSpec((ROW_TILE, w), lambda i: (i, 0))
    full = lambda a: pl.BlockSpec(a.shape, lambda i: (0,) * a.ndim)
    args = (x2, pre_g.reshape(1, D_MODEL), wm, wg, wup, b_gk.reshape(1, GLA_QK))
    return pl.pallas_call(
        _gla_proj_kernel,
        grid=(T // ROW_TILE,),
        in_specs=[row(D_MODEL)] + [full(a) for a in args[1:]],
        out_specs=[row(GLA_QK), row(GLA_QK), row(GLA_V), row(GLA_QK), row(GLA_V)],
        out_shape=[jax.ShapeDtypeStruct((T, GLA_QK), F32), jax.ShapeDtypeStruct((T, GLA_QK), F32),
                   jax.ShapeDtypeStruct((T, GLA_V), BF16), jax.ShapeDtypeStruct((T, GLA_QK), F32),
                   jax.ShapeDtypeStruct((T, GLA_V), BF16)],
        compiler_params=_params("parallel"),
        name="gla_proj",
    )(*args)


def _gla_core_kernel(q_ref, k_ref, v_ref, gk_ref, zs_ref, x_ref, hn_ref, pn_ref, wo_ref,
                     o_ref, state_ref, og_ref):
    C = GLA_CHUNK

    @pl.when(pl.program_id(1) == 0)
    def _():
        state_ref[...] = jnp.zeros_like(state_ref)

    r = lax.broadcasted_iota(jnp.int32, (C, C), 0)
    c = lax.broadcasted_iota(jnp.int32, (C, C), 1)
    tril = r >= c
    lmat = jnp.where(tril, 1.0, 0.0).astype(BF16)
    scale = GLA_DK ** -0.5
    hn = hn_ref[...]
    for ci in range(GLA_SEQ_TILE // C):
        rows = slice(ci * C, (ci + 1) * C)
        g = gk_ref[0, rows, :]
        g_hi = g.astype(BF16)
        g_lo = (g - g_hi.astype(F32)).astype(BF16)
        b_all = _dot(lmat, g_hi) + _dot(lmat, g_lo)
        for h in range(GLA_HEADS):
            ksl = slice(h * GLA_DK, (h + 1) * GLA_DK)
            vsl = slice(h * GLA_DV, (h + 1) * GLA_DV)
            b = b_all[:, ksl]
            b_last = b[C - 1:C, :]
            eb = jnp.exp(b)
            kh = k_ref[0, rows, ksl]
            qe = (q_ref[0, rows, ksl] * scale * eb).astype(BF16)
            ke = (kh * jnp.exp(-b)).astype(BF16)
            kd = kh * jnp.exp(b_last - b)
            a = jnp.where(tril, _dot_nt(qe, ke), 0.0).astype(BF16)
            vh = v_ref[0, rows, vsl]
            st = state_ref[h]
            o = _dot(a, vh) + _dot(qe, st.astype(BF16))
            kd_t = jnp.transpose(kd).astype(BF16)
            dec = jnp.transpose(eb)[:, C - 1:C]
            state_ref[h] = dec * st + _dot(kd_t, vh)
            og_ref[rows, vsl] = _rms(o, hn)
    u = (og_ref[...] * zs_ref[0].astype(F32)).astype(BF16)
    y = _dot(u, wo_ref[...])
    o_ref[0] = x_ref[0] + _rms(y, pn_ref[...])


def _gla_core(q, k, v, gk, zs, x, head_norm, post_g, w_out):
    B, S, _ = x.shape
    ts = GLA_SEQ_TILE
    blk = lambda w: pl.BlockSpec((1, ts, w), lambda b, s: (b, s, 0))
    full = lambda a: pl.BlockSpec(a.shape, lambda b, s: (0,) * a.ndim)
    args = (q, k, v, gk, zs, x, head_norm.reshape(1, GLA_DV), post_g.reshape(1, D_MODEL), w_out.astype(BF16))
    return pl.pallas_call(
        _gla_core_kernel,
        grid=(B, S // ts),
        in_specs=[blk(GLA_QK), blk(GLA_QK), blk(GLA_V), blk(GLA_QK), blk(GLA_V), blk(D_MODEL)]
        + [full(a) for a in args[6:]],
        out_specs=blk(D_MODEL),
        out_shape=jax.ShapeDtypeStruct((B, S, D_MODEL), F32),
        scratch_shapes=[pltpu.VMEM((GLA_HEADS, GLA_DK, GLA_DV), F32), pltpu.VMEM((ts, GLA_V), F32)],
        compiler_params=_params("parallel", "arbitrary"),
        name="gla_core",
    )(*args)


def _nsa_proj_kernel(x_ref, pos_ref, g_ref, inv_ref, sgn_ref, wm_ref, wg_ref, bg_ref,
                     q_ref, kc_ref, vc_ref, ks_ref, vs_ref, kw_ref, vw_ref, gt_ref, zs_ref):
    h = _rms(x_ref[...], g_ref[...]).astype(BF16)
    ang = pos_ref[...].astype(F32) * inv_ref[...]
    cos = jnp.cos(ang)
    sin = jnp.sin(ang) * sgn_ref[...]
    first_half = sgn_ref[...] < 0.0

    def rope(y):
        partner = jnp.where(first_half, pltpu.roll(y, LANES - NSA_DH // 2, axis=1),
                            pltpu.roll(y, NSA_DH // 2, axis=1))
        return y * cos + partner * sin

    def seg(off, width):
        return _dot(h, wm_ref[:, off:off + width])

    q_scale = NSA_DH ** -0.5
    for j in range(NSA_Q // LANES):
        q_ref[:, j * LANES:(j + 1) * LANES] = (rope(seg(j * LANES, LANES)) * q_scale).astype(BF16)
    off = NSA_Q
    for ref, roped in ((kc_ref, True), (vc_ref, False), (ks_ref, True), (vs_ref, False),
                       (kw_ref, True), (vw_ref, False)):
        for j in range(NSA_KV // LANES):
            y = seg(off + j * LANES, LANES)
            if roped:
                y = rope(y)
            ref[:, j * LANES:(j + 1) * LANES] = y.astype(ref.dtype)
        off += NSA_KV
    zs_ref[...] = _silu(seg(off, NSA_Q)).astype(BF16)
    gt_ref[...] = jax.nn.sigmoid(_dot(h, wg_ref[...]) + bg_ref[...])


def _nsa_proj(x2, pos2, pre_g, w_in, b_gate):
    T = x2.shape[0]
    o_gl = NSA_Q + 6 * NSA_KV
    n_gl = 3 * NSA_HEADS
    wm = jnp.concatenate([w_in[:, :o_gl], w_in[:, o_gl + n_gl:]], axis=1).astype(BF16)
    wg = jnp.pad(w_in[:, o_gl:o_gl + n_gl], ((0, 0), (0, LANES - n_gl))).astype(BF16)
    bg = jnp.pad(b_gate, (0, LANES - n_gl)).reshape(1, LANES)
    half = NSA_DH // 2
    inv = ROPE_THETA ** (-jnp.arange(0, NSA_DH, 2, dtype=F32) / NSA_DH)
    inv = jnp.tile(inv, LANES // half).reshape(1, LANES)
    sgn = jnp.where((jnp.arange(LANES) % NSA_DH) < half, -1.0, 1.0).astype(F32).reshape(1, LANES)
    row = lambda w: pl.BlockSpec((ROW_TILE, w), lambda i: (i, 0))
    full = lambda a: pl.BlockSpec(a.shape, lambda i: (0,) * a.ndim)
    args = (x2, pos2, pre_g.reshape(1, D_MODEL), inv, sgn, wm, wg, bg)
    sds = jax.ShapeDtypeStruct
    return pl.pallas_call(
        _nsa_proj_kernel,
        grid=(T // ROW_TILE,),
        in_specs=[row(D_MODEL), row(1)] + [full(a) for a in args[2:]],
        out_specs=[row(NSA_Q)] + [row(NSA_KV)] * 6 + [row(LANES), row(NSA_Q)],
        out_shape=[sds((T, NSA_Q), BF16), sds((T, NSA_KV), F32), sds((T, NSA_KV), F32),
                   sds((T, NSA_KV), BF16), sds((T, NSA_KV), BF16), sds((T, NSA_KV), BF16),
                   sds((T, NSA_KV), BF16), sds((T, LANES), F32), sds((T, NSA_Q), BF16)],
        compiler_params=_params("parallel"),
        name="nsa_proj",
    )(*args)


def _compress_kernel(p_ref, pe_ref, w1_ref, w2_ref, o_ref):
    half = CMP_STRIDE * NSA_DH
    p = p_ref[0, 0].astype(BF16)
    a = _dot(p, w1_ref[0:half, :])
    b = _dot(p, w1_ref[half:, :])
    bias = _dot(pe_ref[...].astype(BF16), w1_ref[...])[0:1]
    n = p.shape[0]
    hid = a + pltpu.roll(b, n - 1, axis=0) + bias
    o_ref[0, 0] = _dot(_silu(hid).astype(BF16), w2_ref[...])


def _compress(t, pe, w1, w2):
    B, S, _ = t.shape
    n = S // CMP_STRIDE
    pieces = t.reshape(B, n, CMP_STRIDE, NSA_GROUPS, NSA_DH).transpose(0, 3, 1, 2, 4)
    pieces = pieces.reshape(B, NSA_GROUPS, n, CMP_STRIDE * NSA_DH)
    pe_flat = jnp.pad(pe.reshape(1, CMP_BLOCK * NSA_DH), ((0, 7), (0, 0)))
    full = lambda a: pl.BlockSpec(a.shape, lambda b, g: (0,) * a.ndim)
    args = (pieces, pe_flat, w1.astype(BF16), w2.astype(BF16))
    return pl.pallas_call(
        _compress_kernel,
        grid=(B, NSA_GROUPS),
        in_specs=[pl.BlockSpec((1, 1, n, CMP_STRIDE * NSA_DH), lambda b, g: (b, g, 0, 0))]
        + [full(a) for a in args[1:]],
        out_specs=pl.BlockSpec((1, 1, n, NSA_DH), lambda b, g: (b, g, 0, 0)),
        out_shape=jax.ShapeDtypeStruct((B, NSA_GROUPS, n, NSA_DH), F32),
        compiler_params=_params("parallel", "parallel"),
        name="nsa_compress",
    )(*args)


def _softmax_step(s, v, m, l, acc):
    m_new = jnp.maximum(m, jnp.max(s, axis=-1, keepdims=True))
    alpha = jnp.exp(m - m_new)
    p = jnp.exp(s - m_new)
    l = alpha * l + jnp.sum(p, axis=-1, keepdims=True)
    acc = alpha * acc + _dot(p.astype(BF16), v)
    return m_new, l, acc


def _nsa_attn_kernel(q_ref, kct_ref, vcm_ref, kst_ref, vs_ref, kwt_ref, vw_ref, gt_ref, ov_ref,
                     o_ref, *, n_cmp, n_blk):
    tq = ATT_TILE
    qi = pl.program_id(2)
    t0 = qi * tq
    t_col = t0 + lax.broadcasted_iota(jnp.int32, (tq, 1), 0)
    qall = q_ref[0]

    c_idx = lax.broadcasted_iota(jnp.int32, (1, n_cmp), 1)
    cvalid = (c_idx * CMP_STRIDE + (CMP_BLOCK - 1)) <= t_col
    anyvalid = jnp.where(t_col >= CMP_BLOCK - 1, 1.0, 0.0)
    kct = kct_ref[0, 0]
    vcm = vcm_ref[0, 0]
    psum = jnp.zeros((tq, n_cmp), F32)
    o_cmp = []
    for h in range(NSA_HPG):
        qh = qall[:, h * NSA_DH:(h + 1) * NSA_DH]
        s = jnp.where(cvalid, _dot(qh, kct), NEG)
        e = jnp.exp(s - jnp.max(s, axis=-1, keepdims=True))
        p = e / jnp.sum(e, axis=-1, keepdims=True) * anyvalid
        psum = psum + p
        o_cmp.append(_dot(p.astype(BF16), vcm))
    p_hi = psum.astype(BF16)
    p_lo = (psum - p_hi.astype(F32)).astype(BF16)
    ov = ov_ref[...]
    imp_t = jnp.transpose(_dot(p_hi, ov) + _dot(p_lo, ov))[0:n_blk]

    nblk = n_blk
    j_idx = lax.broadcasted_iota(jnp.int32, (nblk, 1), 0)
    t_row = t0 + lax.broadcasted_iota(jnp.int32, (1, tq), 1)
    cur = t_row // SEL_BLOCK
    forced = (j_idx == 0) | (j_idx == cur) | (j_idx == cur - 1)
    causal_blk = j_idx * SEL_BLOCK <= t_row
    val = jnp.where(forced, BIG, jnp.where(causal_blk, imp_t, NEG))
    rank = jnp.zeros((nblk, tq), F32)
    j_full = lax.broadcasted_iota(jnp.int32, (nblk, tq), 0)
    for k in range(nblk):
        vk = val[k:k + 1, :]
        ge = jnp.where(vk >= val, 1.0, 0.0)
        gt = jnp.where(vk > val, 1.0, 0.0)
        rank = rank + jnp.where(j_full > k, ge, gt)
    sel_bias_t = jnp.where(rank < float(SEL_TOP), 0.0, NEG)
    sel_bias_t = jnp.concatenate([sel_bias_t, jnp.zeros((LANES - nblk, tq), F32)], axis=0)
    sel_bias = jnp.transpose(sel_bias_t)[:, 0:nblk].astype(BF16)

    k_lane = lax.broadcasted_iota(jnp.int32, (1, tq), 1)
    diag_bias = jnp.where(t0 + k_lane <= t_col, 0.0, NEG)
    win_bias = []
    for d in range(WINDOW // tq + 1):
        pos = (qi - d) * tq + k_lane
        ok = (pos <= t_col) & (pos > t_col - WINDOW) & (qi - d >= 0)
        win_bias.append(jnp.where(ok, 0.0, NEG))

    gates = gt_ref[0, 0]
    outs = []
    for h in range(NSA_HPG):
        qh = qall[:, h * NSA_DH:(h + 1) * NSA_DH]
        q_ext = jnp.concatenate([qh, sel_bias], axis=1)
        init = (jnp.full((tq, 1), NEG, F32), jnp.zeros((tq, 1), F32), jnp.zeros((tq, NSA_DH), F32))

        def sel_body(kt, carry):
            return _softmax_step(_dot(q_ext, kst_ref[0, 0, kt]), vs_ref[0, 0, kt], *carry)

        carry = lax.fori_loop(0, qi, sel_body, init)
        m, l, acc = _softmax_step(_dot(q_ext, kst_ref[0, 0, qi]) + diag_bias, vs_ref[0, 0, qi], *carry)
        o_sel = acc / l

        carry = init
        for d in range(WINDOW // tq, -1, -1):
            kt = jnp.maximum(qi - d, 0)
            carry = _softmax_step(_dot(qh, kwt_ref[0, 0, kt]) + win_bias[d], vw_ref[0, 0, kt], *carry)
        o_win = carry[2] / carry[1]

        g0 = gates[:, 3 * h:3 * h + 1]
        g1 = gates[:, 3 * h + 1:3 * h + 2]
        g2 = gates[:, 3 * h + 2:3 * h + 3]
        outs.append(g0 * o_cmp[h] + g1 * o_sel + g2 * o_win)
    o_ref[0] = jnp.concatenate(outs, axis=1).astype(BF16)


def _nsa_attn(q, kcmp, vcmp, ks, vs, kw, vw, gates):
    B, S, _ = q.shape
    G, DH, tq = NSA_GROUPS, NSA_DH, ATT_TILE
    n_cmp = kcmp.shape[2]
    nt = S // tq
    ns = S // SEL_BLOCK

    def keys_t(k, with_blocks):
        kt = k.reshape(B, nt, tq, G, DH).transpose(0, 3, 1, 4, 2)
        if with_blocks:
            onehot = (jnp.arange(S)[None, :] // SEL_BLOCK == jnp.arange(ns)[:, None]).astype(BF16)
            onehot = onehot.reshape(ns, nt, tq).transpose(1, 0, 2)
            kt = jnp.concatenate([kt, jnp.broadcast_to(onehot, (B, G, nt, ns, tq))], axis=3)
        return kt

    def vals(v):
        return v.reshape(B, nt, tq, G, DH).transpose(0, 3, 1, 2, 4)

    kct = kcmp.transpose(0, 1, 3, 2).astype(BF16)
    vcm = vcmp.astype(BF16)
    kst, kwt = keys_t(ks, True), keys_t(kw, False)
    vst, vwt = vals(vs), vals(vw)
    gt = gates[:, :, :3 * NSA_HEADS].reshape(B, S, G, 3 * NSA_HPG).transpose(0, 2, 1, 3)
    gt = jnp.pad(gt, ((0, 0), (0, 0), (0, 0), (0, LANES - 3 * NSA_HPG)))
    cs = jnp.arange(n_cmp)[:, None] * CMP_STRIDE
    ss = jnp.arange(LANES)[None, :] * SEL_BLOCK
    n_valid = (S - CMP_BLOCK) // CMP_STRIDE + 1
    ov = ((cs < ss + SEL_BLOCK) & (cs + CMP_BLOCK > ss) & (jnp.arange(n_cmp)[:, None] < n_valid)
          & (jnp.arange(LANES)[None, :] < ns)).astype(BF16)

    per_bg = lambda a: pl.BlockSpec((1, 1) + a.shape[2:], lambda b, g, i: (b, g) + (0,) * (a.ndim - 2))
    return pl.pallas_call(
        functools.partial(_nsa_attn_kernel, n_cmp=n_cmp, n_blk=ns),
        grid=(B, G, nt),
        in_specs=[pl.BlockSpec((1, tq, NSA_HPG * DH), lambda b, g, i: (b, i, g)),
                  per_bg(kct), per_bg(vcm), per_bg(kst), per_bg(vst), per_bg(kwt), per_bg(vwt),
                  pl.BlockSpec((1, 1, tq, LANES), lambda b, g, i: (b, g, i, 0)),
                  pl.BlockSpec(ov.shape, lambda b, g, i: (0, 0))],
        out_specs=pl.BlockSpec((1, tq, NSA_HPG * DH), lambda b, g, i: (b, i, g)),
        out_shape=jax.ShapeDtypeStruct((B, S, NSA_Q), BF16),
        compiler_params=_params("parallel", "parallel", "arbitrary"),
        name="nsa_attn",
    )(q, kct, vcm, kst, vst, kwt, vwt, gt, ov)


def _nsa_out_kernel(og_ref, zs_ref, x_ref, pn_ref, wo_ref, o_ref):
    u = (og_ref[...].astype(F32) * zs_ref[...].astype(F32)).astype(BF16)
    o_ref[...] = x_ref[...] + _rms(_dot(u, wo_ref[...]), pn_ref[...])


def _nsa_out(og, zs, x2, post_g, w_out):
    T = x2.shape[0]
    row = lambda w: pl.BlockSpec((ROW_TILE, w), lambda i: (i, 0))
    full = lambda a: pl.BlockSpec(a.shape, lambda i: (0,) * a.ndim)
    args = (og, zs, x2, post_g.reshape(1, D_MODEL), w_out.astype(BF16))
    return pl.pallas_call(
        _nsa_out_kernel,
        grid=(T // ROW_TILE,),
        in_specs=[row(NSA_Q), row(NSA_Q), row(D_MODEL)] + [full(a) for a in args[3:]],
        out_specs=row(D_MODEL),
        out_shape=jax.ShapeDtypeStruct((T, D_MODEL), F32),
        compiler_params=_params("parallel"),
        name="nsa_out",
    )(*args)


def _gla_layer(x, pre_g, post_g, w_in, w_gk_up, b_gk, head_norm, w_out):
    B, S, D = x.shape
    q, k, v, gk, zs = _gla_proj(x.reshape(B * S, D), pre_g, w_in, w_gk_up, b_gk)
    r3 = lambda a: a.reshape(B, S, a.shape[-1])
    return _gla_core(r3(q), r3(k), r3(v), r3(gk), r3(zs), x, head_norm, post_g, w_out)


def _nsa_layer(x, positions, pre_g, post_g, w_in, b_gate, pe_k, pe_v, ck_w1, ck_w2, cv_w1, cv_w2, w_out):
    B, S, D = x.shape
    x2 = x.reshape(B * S, D)
    q, kc, vc, ks, vs, kw, vw, gates, zs = _nsa_proj(x2, positions.reshape(B * S, 1), pre_g, w_in, b_gate)
    r3 = lambda a: a.reshape(B, S, a.shape[-1])
    kcmp = _compress(r3(kc), pe_k, ck_w1, ck_w2)
    vcmp = _compress(r3(vc), pe_v, cv_w1, cv_w2)
    og = _nsa_attn(r3(q), kcmp, vcmp, r3(ks), r3(vs), r3(kw), r3(vw), r3(gates))
    return _nsa_out(og.reshape(B * S, NSA_Q), zs, x2, post_g, w_out).reshape(B, S, D)


def kernel(x, positions, pre_norm, post_norm, gla_w_in, gla_w_gk_up, gla_b_gk, gla_head_norm, gla_w_out,
           nsa_w_in, nsa_b_gate, nsa_pe_k, nsa_pe_v, nsa_ck_w1, nsa_ck_w2, nsa_cv_w1, nsa_cv_w2, nsa_w_out):
    depth = pre_norm.shape[0]
    for i in range(depth):
        j = i // 2
        if i % 2 == 0:
            x = _gla_layer(x, pre_norm[i], post_norm[i], gla_w_in[j], gla_w_gk_up[j], gla_b_gk[j],
                           gla_head_norm[j], gla_w_out[j])
        else:
            x = _nsa_layer(x, positions, pre_norm[i], post_norm[i], nsa_w_in[j], nsa_b_gate[j],
                           nsa_pe_k[j], nsa_pe_v[j], nsa_ck_w1[j], nsa_ck_w2[j], nsa_cv_w1[j],
                           nsa_cv_w2[j], nsa_w_out[j])
    return x
```

```python
import functools

import jax
import jax.numpy as jnp
from jax import lax
from jax.experimental import pallas as pl
from jax.experimental.pallas import tpu as pltpu

F32 = jnp.float32
BF16 = jnp.bfloat16

D_MODEL = 1024
NORM_EPS = 1e-6
ROPE_THETA = 10000.0
NEG = -1e30
BIG = 1e9
LOG2_E = 1.4426950408889634

GLA_HEADS = 4
GLA_DK = 128
GLA_DV = 256
GLA_QK = GLA_HEADS * GLA_DK
GLA_V = GLA_HEADS * GLA_DV
GLA_GATE_RANK = 16
GLA_GATE_NORMALIZER = 16.0
GLA_CHUNK = 64

NSA_HEADS = 16
NSA_GROUPS = 4
NSA_HPG = NSA_HEADS // NSA_GROUPS
NSA_DH = 64
NSA_Q = NSA_HEADS * NSA_DH
NSA_KV = NSA_GROUPS * NSA_DH
CMP_BLOCK = 32
CMP_STRIDE = 16
CMP_HIDDEN = 256
SEL_BLOCK = 64
SEL_TOP = 16
WINDOW = 512

LANES = 128
ROW_TILE = 512
GLA_SEQ_TILE = 256
ATT_TILE = 256
VMEM_LIMIT = 56 * 1024 * 1024


def _dot(a, b):
    return jnp.dot(a, b, preferred_element_type=F32)


def _dot_nt(a, b):
    return lax.dot_general(a, b, (((1,), (1,)), ((), ())), preferred_element_type=F32)


def _rms(x, g):
    return x * lax.rsqrt(jnp.mean(x * x, axis=-1, keepdims=True) + NORM_EPS) * g


def _silu(z):
    return z * jax.nn.sigmoid(z)


def _params(*sem):
    return pltpu.CompilerParams(dimension_semantics=sem, vmem_limit_bytes=VMEM_LIMIT)


def _gla_proj_kernel(x_ref, g_ref, wm_ref, wg_ref, wup_ref, bgk_ref,
                     q_ref, k_ref, v_ref, gk_ref, zs_ref):
    h = _rms(x_ref[...], g_ref[...]).astype(BF16)
    q_ref[...] = _dot(h, wm_ref[:, 0:GLA_QK])
    k_ref[...] = _dot(h, wm_ref[:, GLA_QK:2 * GLA_QK])
    v_ref[...] = _dot(h, wm_ref[:, 2 * GLA_QK:2 * GLA_QK + GLA_V]).astype(BF16)
    zs_ref[...] = _silu(_dot(h, wm_ref[:, 2 * GLA_QK + GLA_V:])).astype(BF16)
    glr = _dot(h, wg_ref[...])
    pre = _dot(glr.astype(BF16), wup_ref[...]) + bgk_ref[...]
    gk_ref[...] = (jnp.minimum(pre, 0.0) - jnp.log(1.0 + jnp.exp(-jnp.abs(pre)))) / GLA_GATE_NORMALIZER


def _gla_proj(x2, pre_g, w_in, w_gk_up, b_gk):
    T = x2.shape[0]
    o_gl = 2 * GLA_QK + GLA_V
    wm = jnp.concatenate([w_in[:, :o_gl], w_in[:, o_gl + GLA_GATE_RANK:]], axis=1).astype(BF16)
    wg = jnp.pad(w_in[:, o_gl:o_gl + GLA_GATE_RANK], ((0, 0), (0, LANES - GLA_GATE_RANK))).astype(BF16)
    wup = jnp.pad(w_gk_up, ((0, LANES - GLA_GATE_RANK), (0, 0))).astype(BF16)
    row = lambda w: pl.BlockSpec((ROW_TILE, w), lambda i: (i, 0))
    full = lambda a: pl.BlockSpec(a.shape, lambda i: (0,) * a.ndim)
    args = (x2, pre_g.reshape(1, D_MODEL), wm, wg, wup, b_gk.reshape(1, GLA_QK))
    return pl.pallas_call(
        _gla_proj_kernel,
        grid=(T // ROW_TILE,),
        in_specs=[row(D_MODEL)] + [full(a) for a in args[1:]],
        out_specs=[row(GLA_QK), row(GLA_QK), row(GLA_V), row(GLA_QK), row(GLA_V)],
        out_shape=[jax.ShapeDtypeStruct((T, GLA_QK), F32), jax.ShapeDtypeStruct((T, GLA_QK), F32),
                   jax.ShapeDtypeStruct((T, GLA_V), BF16), jax.ShapeDtypeStruct((T, GLA_QK), F32),
                   jax.ShapeDtypeStruct((T, GLA_V), BF16)],
        compiler_params=_params("parallel"),
        name="gla_proj",
    )(*args)


def _gla_core_kernel(q_ref, k_ref, v_ref, gk_ref, zs_ref, x_ref, hn_ref, pn_ref, wo_ref,
                     o_ref, state_ref, og_ref):
    C = GLA_CHUNK

    @pl.when(pl.program_id(1) == 0)
    def _():
        state_ref[...] = jnp.zeros_like(state_ref)

    r = lax.broadcasted_iota(jnp.int32, (C, C), 0)
    c = lax.broadcasted_iota(jnp.int32, (C, C), 1)
    tril = r >= c
    lmat = jnp.where(tril, 1.0, 0.0).astype(BF16)
    scale = GLA_DK ** -0.5
    hn = hn_ref[...]
    for ci in range(GLA_SEQ_TILE // C):
        rows = slice(ci * C, (ci + 1) * C)
        g = gk_ref[0, rows, :]
        g_hi = g.astype(BF16)
        g_lo = (g - g_hi.astype(F32)).astype(BF16)
        b_all = _dot(lmat, g_hi) + _dot(lmat, g_lo)
        for h in range(GLA_HEADS):
            ksl = slice(h * GLA_DK, (h + 1) * GLA_DK)
            vsl = slice(h * GLA_DV, (h + 1) * GLA_DV)
            b = b_all[:, ksl]
            b_last = b[C - 1:C, :]
            eb = jnp.exp(b)
            kh = k_ref[0, rows, ksl]
            qe = (q_ref[0, rows, ksl] * scale * eb).astype(BF16)
            ke = (kh * jnp.exp(-b)).astype(BF16)
            kd = kh * jnp.exp(b_last - b)
            a = jnp.where(tril, _dot_nt(qe, ke), 0.0).astype(BF16)
            vh = v_ref[0, rows, vsl]
            st = state_ref[h]
            o = _dot(a, vh) + _dot(qe, st.astype(BF16))
            kd_t = jnp.transpose(kd).astype(BF16)
            dec = jnp.transpose(eb)[:, C - 1:C]
            state_ref[h] = dec * st + _dot(kd_t, vh)
            og_ref[rows, vsl] = _rms(o, hn)
    u = (og_ref[...] * zs_ref[0].astype(F32)).astype(BF16)
    y = _dot(u, wo_ref[...])
    o_ref[0] = x_ref[0] + _rms(y, pn_ref[...])


def _gla_core(q, k, v, gk, zs, x, head_norm, post_g, w_out):
    B, S, _ = x.shape
    ts = GLA_SEQ_TILE
    blk = lambda w: pl.BlockSpec((1, ts, w), lambda b, s: (b, s, 0))
    full = lambda a: pl.BlockSpec(a.shape, lambda b, s: (0,) * a.ndim)
    args = (q, k, v, gk, zs, x, head_norm.reshape(1, GLA_DV), post_g.reshape(1, D_MODEL), w_out.astype(BF16))
    return pl.pallas_call(
        _gla_core_kernel,
        grid=(B, S // ts),
        in_specs=[blk(GLA_QK), blk(GLA_QK), blk(GLA_V), blk(GLA_QK), blk(GLA_V), blk(D_MODEL)]
        + [full(a) for a in args[6:]],
        out_specs=blk(D_MODEL),
        out_shape=jax.ShapeDtypeStruct((B, S, D_MODEL), F32),
        scratch_shapes=[pltpu.VMEM((GLA_HEADS, GLA_DK, GLA_DV), F32), pltpu.VMEM((ts, GLA_V), F32)],
        compiler_params=_params("parallel", "arbitrary"),
        name="gla_core",
    )(*args)


def _nsa_proj_kernel(x_ref, pos_ref, g_ref, inv_ref, sgn_ref, wm_ref, wg_ref, bg_ref,
                     q_ref, kc_ref, vc_ref, ks_ref, vs_ref, kw_ref, vw_ref, gt_ref, zs_ref):
    h = _rms(x_ref[...], g_ref[...]).astype(BF16)
    ang = pos_ref[...].astype(F32) * inv_ref[...]
    cos = jnp.cos(ang)
    sin = jnp.sin(ang) * sgn_ref[...]
    first_half = sgn_ref[...] < 0.0

    def rope(y):
        partner = jnp.where(first_half, pltpu.roll(y, LANES - NSA_DH // 2, axis=1),
                            pltpu.roll(y, NSA_DH // 2, axis=1))
        return y * cos + partner * sin

    def seg(off, width):
        return _dot(h, wm_ref[:, off:off + width])

    q_scale = NSA_DH ** -0.5 * LOG2_E
    for j in range(NSA_Q // LANES):
        q_ref[:, j * LANES:(j + 1) * LANES] = (rope(seg(j * LANES, LANES)) * q_scale).astype(BF16)
    off = NSA_Q
    for ref, roped in ((kc_ref, True), (vc_ref, False), (ks_ref, True), (vs_ref, False),
                       (kw_ref, True), (vw_ref, False)):
        for j in range(NSA_KV // LANES):
            y = seg(off + j * LANES, LANES)
            if roped:
                y = rope(y)
            ref[:, j * LANES:(j + 1) * LANES] = y.astype(ref.dtype)
        off += NSA_KV
    zs_ref[...] = _silu(seg(off, NSA_Q)).astype(BF16)
    gt_ref[...] = jax.nn.sigmoid(_dot(h, wg_ref[...]) + bg_ref[...])


def _nsa_proj(x2, pos2, pre_g, w_in, b_gate):
    T = x2.shape[0]
    o_gl = NSA_Q + 6 * NSA_KV
    n_gl = 3 * NSA_HEADS
    wm = jnp.concatenate([w_in[:, :o_gl], w_in[:, o_gl + n_gl:]], axis=1).astype(BF16)
    wg = jnp.pad(w_in[:, o_gl:o_gl + n_gl], ((0, 0), (0, LANES - n_gl))).astype(BF16)
    bg = jnp.pad(b_gate, (0, LANES - n_gl)).reshape(1, LANES)
    half = NSA_DH // 2
    inv = ROPE_THETA ** (-jnp.arange(0, NSA_DH, 2, dtype=F32) / NSA_DH)
    inv = jnp.tile(inv, LANES // half).reshape(1, LANES)
    sgn = jnp.where((jnp.arange(LANES) % NSA_DH) < half, -1.0, 1.0).astype(F32).reshape(1, LANES)
    row = lambda w: pl.BlockSpec((ROW_TILE, w), lambda i: (i, 0))
    full = lambda a: pl.BlockSpec(a.shape, lambda i: (0,) * a.ndim)
    args = (x2, pos2, pre_g.reshape(1, D_MODEL), inv, sgn, wm, wg, bg)
    sds = jax.ShapeDtypeStruct
    return pl.pallas_call(
        _nsa_proj_kernel,
        grid=(T // ROW_TILE,),
        in_specs=[row(D_MODEL), row(1)] + [full(a) for a in args[2:]],
        out_specs=[row(NSA_Q)] + [row(NSA_KV)] * 6 + [row(LANES), row(NSA_Q)],
        out_shape=[sds((T, NSA_Q), BF16), sds((T, NSA_KV), F32), sds((T, NSA_KV), F32),
                   sds((T, NSA_KV), BF16), sds((T, NSA_KV), BF16), sds((T, NSA_KV), BF16),
                   sds((T, NSA_KV), BF16), sds((T, LANES), F32), sds((T, NSA_Q), BF16)],
        compiler_params=_params("parallel"),
        name="nsa_proj",
    )(*args)


def _compress_kernel(p_ref, pe_ref, w1_ref, w2_ref, o_ref):
    half = CMP_STRIDE * NSA_DH
    p = p_ref[0, 0].astype(BF16)
    a = _dot(p, w1_ref[0:half, :])
    b = _dot(p, w1_ref[half:, :])
    bias = _dot(pe_ref[...].astype(BF16), w1_ref[...])[0:1]
    n = p.shape[0]
    hid = a + pltpu.roll(b, n - 1, axis=0) + bias
    o_ref[0, 0] = _dot(_silu(hid).astype(BF16), w2_ref[...])


def _compress(t, pe, w1, w2):
    B, S, _ = t.shape
    n = S // CMP_STRIDE
    pieces = t.reshape(B, n, CMP_STRIDE, NSA_GROUPS, NSA_DH).transpose(0, 3, 1, 2, 4)
    pieces = pieces.reshape(B, NSA_GROUPS, n, CMP_STRIDE * NSA_DH)
    pe_flat = jnp.pad(pe.reshape(1, CMP_BLOCK * NSA_DH), ((0, 7), (0, 0)))
    full = lambda a: pl.BlockSpec(a.shape, lambda b, g: (0,) * a.ndim)
    args = (pieces, pe_flat, w1.astype(BF16), w2.astype(BF16))
    return pl.pallas_call(
        _compress_kernel,
        grid=(B, NSA_GROUPS),
        in_specs=[pl.BlockSpec((1, 1, n, CMP_STRIDE * NSA_DH), lambda b, g: (b, g, 0, 0))]
        + [full(a) for a in args[1:]],
        out_specs=pl.BlockSpec((1, 1, n, NSA_DH), lambda b, g: (b, g, 0, 0)),
        out_shape=jax.ShapeDtypeStruct((B, NSA_GROUPS, n, NSA_DH), F32),
        compiler_params=_params("parallel", "parallel"),
        name="nsa_compress",
    )(*args)


V_ROWS = NSA_DH + 16


def _nsa_attn_kernel(q_ref, kcm_ref, vct_ref, kse_ref, vst_ref, kwn_ref, vwt_ref, gt_ref, ovt_ref,
                     o_ref, m_ref, acc_ref, *, n_cmp, n_blk):
    tq = ATT_TILE
    qi = pl.program_id(2)
    t0 = qi * tq
    t_row = t0 + lax.broadcasted_iota(jnp.int32, (1, tq), 1)
    q_t = jnp.transpose(q_ref[0].astype(F32)).astype(BF16)
    q_heads = [q_t[h * NSA_DH:(h + 1) * NSA_DH] for h in range(NSA_HPG)]

    c_idx = lax.broadcasted_iota(jnp.int32, (n_cmp, 1), 0)
    cvalid = (c_idx * CMP_STRIDE + (CMP_BLOCK - 1)) <= t_row
    anyvalid = jnp.where(t_row >= CMP_BLOCK - 1, 1.0, 0.0)
    kcm = kcm_ref[0, 0]
    vct = vct_ref[0, 0]
    psum = jnp.zeros((n_cmp, tq), F32)
    o_cmp = []
    for h in range(NSA_HPG):
        s = jnp.where(cvalid, _dot(kcm, q_heads[h]), NEG)
        e = jnp.exp2(s - jnp.max(s, axis=0, keepdims=True))
        p = e * (anyvalid / jnp.sum(e, axis=0, keepdims=True))
        psum = psum + p
        o_cmp.append(_dot(vct, p.astype(BF16)))
    p_hi = psum.astype(BF16)
    p_lo = (psum - p_hi.astype(F32)).astype(BF16)
    ovt = ovt_ref[...]
    imp = _dot(ovt, p_hi) + _dot(ovt, p_lo)

    j_full = lax.broadcasted_iota(jnp.int32, (n_blk, tq), 0)
    cur = t_row // SEL_BLOCK
    forced = (j_full == 0) | (j_full == cur) | (j_full == cur - 1)
    causal_blk = j_full * SEL_BLOCK <= t_row
    val = jnp.where(forced, BIG, jnp.where(causal_blk, imp, NEG))
    rank = jnp.zeros((n_blk, tq), F32)
    for k in range(n_blk):
        vk = val[k:k + 1, :]
        ge = jnp.where(vk >= val, 1.0, 0.0)
        gt = jnp.where(vk > val, 1.0, 0.0)
        rank = rank + jnp.where(j_full > k, ge, gt)
    sel_bias = jnp.where(rank < float(SEL_TOP), 0.0, NEG).astype(BF16)

    r_idx = lax.broadcasted_iota(jnp.int32, (tq, tq), 0)
    c_full = lax.broadcasted_iota(jnp.int32, (tq, tq), 1)
    diag_bias = jnp.where(r_idx <= c_full, 0.0, NEG)
    far_bias = jnp.where(r_idx > c_full, 0.0, NEG)

    def reset():
        m_ref[...] = jnp.full(m_ref.shape, NEG, F32)
        acc_ref[...] = jnp.zeros(acc_ref.shape, F32)

    def tile_step(k_tile, vt_tile, ws, bias):
        scores = [_dot(k_tile, w) for w in ws]
        for h, s in enumerate(scores):
            if bias is not None:
                s = s + bias
            m_old = m_ref[h]
            m_new = jnp.maximum(m_old, jnp.max(s, axis=0, keepdims=True))
            p = jnp.exp2(s - m_new).astype(BF16)
            acc_ref[h] = jnp.exp2(m_old - m_new) * acc_ref[h] + _dot(vt_tile, p)
            m_ref[h] = m_new

    def result(h):
        acc = acc_ref[h]
        return acc[0:NSA_DH] / acc[NSA_DH:NSA_DH + 1]

    w_sel = [jnp.concatenate([q_heads[h], sel_bias], axis=0) for h in range(NSA_HPG)]
    reset()

    def sel_body(kt, carry):
        tile_step(kse_ref[0, 0, kt], vst_ref[0, 0, kt], w_sel, None)
        return carry

    lax.fori_loop(0, qi, sel_body, 0)
    tile_step(kse_ref[0, 0, qi], vst_ref[0, 0, qi], w_sel, diag_bias)
    o_sel = [result(h) for h in range(NSA_HPG)]

    reset()
    for d, bias in ((2, far_bias), (1, None)):
        @pl.when(qi >= d)
        def _(d=d, bias=bias):
            tile_step(kwn_ref[0, 0, qi - d], vwt_ref[0, 0, qi - d], q_heads, bias)
    tile_step(kwn_ref[0, 0, qi], vwt_ref[0, 0, qi], q_heads, diag_bias)
    o_win = [result(h) for h in range(NSA_HPG)]

    gates = gt_ref[0, 0]
    outs = []
    for h in range(NSA_HPG):
        outs.append(gates[3 * h:3 * h + 1] * o_cmp[h] + gates[3 * h + 1:3 * h + 2] * o_sel[h]
                    + gates[3 * h + 2:3 * h + 3] * o_win[h])
    o_ref[0] = jnp.transpose(jnp.concatenate(outs, axis=0)).astype(BF16)


def _nsa_attn(q, kcmp, vcmp, ks, vs, kw, vw, gates):
    B, S, _ = q.shape
    G, DH, tq = NSA_GROUPS, NSA_DH, ATT_TILE
    assert WINDOW == 2 * tq
    n_cmp = kcmp.shape[2]
    nt = S // tq
    ns = S // SEL_BLOCK

    def keys(k):
        return k.reshape(B, nt, tq, G, DH).transpose(0, 3, 1, 2, 4)

    def vals_t(v):
        vt = v.reshape(B, nt, tq, G, DH).transpose(0, 3, 1, 4, 2)
        return jnp.concatenate([vt, jnp.ones((B, G, nt, V_ROWS - DH, tq), BF16)], axis=3)

    onehot = (jnp.arange(S)[:, None] // SEL_BLOCK == jnp.arange(ns)[None, :]).astype(BF16)
    kse = jnp.concatenate([keys(ks), jnp.broadcast_to(onehot.reshape(nt, tq, ns), (B, G, nt, tq, ns))], axis=4)
    kwn = keys(kw)
    vst, vwt = vals_t(vs), vals_t(vw)
    kcm = kcmp.astype(BF16)
    vct = vcmp.transpose(0, 1, 3, 2).astype(BF16)
    n_g = 3 * NSA_HPG
    gt = gates[:, :, :3 * NSA_HEADS].reshape(B, S, G, n_g).transpose(0, 2, 3, 1)
    gt = jnp.pad(gt, ((0, 0), (0, 0), (0, 16 - n_g), (0, 0)))
    cs = jnp.arange(n_cmp)[None, :] * CMP_STRIDE
    ss = jnp.arange(ns)[:, None] * SEL_BLOCK
    n_valid = (S - CMP_BLOCK) // CMP_STRIDE + 1
    ovt = ((cs < ss + SEL_BLOCK) & (cs + CMP_BLOCK > ss) & (jnp.arange(n_cmp)[None, :] < n_valid)).astype(BF16)

    per_bg = lambda a: pl.BlockSpec((1, 1) + a.shape[2:], lambda b, g, i: (b, g) + (0,) * (a.ndim - 2))
    return pl.pallas_call(
        functools.partial(_nsa_attn_kernel, n_cmp=n_cmp, n_blk=ns),
        grid=(B, G, nt),
        in_specs=[pl.BlockSpec((1, tq, NSA_HPG * DH), lambda b, g, i: (b, i, g)),
                  per_bg(kcm), per_bg(vct), per_bg(kse), per_bg(vst), per_bg(kwn), per_bg(vwt),
                  pl.BlockSpec((1, 1, 16, tq), lambda b, g, i: (b, g, 0, i)),
                  pl.BlockSpec(ovt.shape, lambda b, g, i: (0, 0))],
        out_specs=pl.BlockSpec((1, tq, NSA_HPG * DH), lambda b, g, i: (b, i, g)),
        out_shape=jax.ShapeDtypeStruct((B, S, NSA_Q), BF16),
        scratch_shapes=[pltpu.VMEM((NSA_HPG, 1, tq), F32), pltpu.VMEM((NSA_HPG, V_ROWS, tq), F32)],
        compiler_params=_params("parallel", "parallel", "arbitrary"),
        name="nsa_attn",
    )(q, kcm, vct, kse, vst, kwn, vwt, gt, ovt)


def _nsa_out_kernel(og_ref, zs_ref, x_ref, pn_ref, wo_ref, o_ref):
    u = (og_ref[...].astype(F32) * zs_ref[...].astype(F32)).astype(BF16)
    o_ref[...] = x_ref[...] + _rms(_dot(u, wo_ref[...]), pn_ref[...])


def _nsa_out(og, zs, x2, post_g, w_out):
    T = x2.shape[0]
    row = lambda w: pl.BlockSpec((ROW_TILE, w), lambda i: (i, 0))
    full = lambda a: pl.BlockSpec(a.shape, lambda i: (0,) * a.ndim)
    args = (og, zs, x2, post_g.reshape(1, D_MODEL), w_out.astype(BF16))
    return pl.pallas_call(
        _nsa_out_kernel,
        grid=(T // ROW_TILE,),
        in_specs=[row(NSA_Q), row(NSA_Q), row(D_MODEL)] + [full(a) for a in args[3:]],
        out_specs=row(D_MODEL),
        out_shape=jax.ShapeDtypeStruct((T, D_MODEL), F32),
        compiler_params=_params("parallel"),
        name="nsa_out",
    )(*args)


def _gla_layer(x, pre_g, post_g, w_in, w_gk_up, b_gk, head_norm, w_out):
    B, S, D = x.shape
    q, k, v, gk, zs = _gla_proj(x.reshape(B * S, D), pre_g, w_in, w_gk_up, b_gk)
    r3 = lambda a: a.reshape(B, S, a.shape[-1])
    return _gla_core(r3(q), r3(k), r3(v), r3(gk), r3(zs), x, head_norm, post_g, w_out)


def _nsa_layer(x, positions, pre_g, post_g, w_in, b_gate, pe_k, pe_v, ck_w1, ck_w2, cv_w1, cv_w2, w_out):
    B, S, D = x.shape
    x2 = x.reshape(B * S, D)
    q, kc, vc, ks, vs, kw, vw, gates, zs = _nsa_proj(x2, positions.reshape(B * S, 1), pre_g, w_in, b_gate)
    r3 = lambda a: a.reshape(B, S, a.shape[-1])
    kcmp = _compress(r3(kc), pe_k, ck_w1, ck_w2)
    vcmp = _compress(r3(vc), pe_v, cv_w1, cv_w2)
    og = _nsa_attn(r3(q), kcmp, vcmp, r3(ks), r3(vs), r3(kw), r3(vw), r3(gates))
    return _nsa_out(og.reshape(B * S, NSA_Q), zs, x2, post_g, w_out).reshape(B, S, D)


def kernel(x, positions, pre_norm, post_norm, gla_w_in, gla_w_gk_up, gla_b_gk, gla_head_norm, gla_w_out,
           nsa_w_in, nsa_b_gate, nsa_pe_k, nsa_pe_v, nsa_ck_w1, nsa_ck_w2, nsa_cv_w1, nsa_cv_w2, nsa_w_out):
    depth = pre_norm.shape[0]
    for i in range(depth):
        j = i // 2
        if i % 2 == 0:
            x = _gla_layer(x, pre_norm[i], post_norm[i], gla_w_in[j], gla_w_gk_up[j], gla_b_gk[j],
                           gla_head_norm[j], gla_w_out[j])
        else:
            x = _nsa_layer(x, positions, pre_norm[i], post_norm[i], nsa_w_in[j], nsa_b_gate[j],
                           nsa_pe_k[j], nsa_pe_v[j], nsa_ck_w1[j], nsa_ck_w2[j], nsa_cv_w1[j],
                           nsa_cv_w2[j], nsa_w_out[j])
    return x
```

```python
import functools

import jax
import jax.numpy as jnp
from jax import lax
from jax.experimental import pallas as pl
from jax.experimental.pallas import tpu as pltpu

F32 = jnp.float32
BF16 = jnp.bfloat16

D_MODEL = 1024
NORM_EPS = 1e-6
ROPE_THETA = 10000.0
NEG = -1e30
BIG = 1e9
LOG2_E = 1.4426950408889634

GLA_HEADS = 4
GLA_DK = 128
GLA_DV = 256
GLA_QK = GLA_HEADS * GLA_DK
GLA_V = GLA_HEADS * GLA_DV
GLA_GATE_RANK = 16
GLA_GATE_NORMALIZER = 16.0
GLA_CHUNK = 64

NSA_HEADS = 16
NSA_GROUPS = 4
NSA_HPG = NSA_HEADS // NSA_GROUPS
NSA_DH = 64
NSA_Q = NSA_HEADS * NSA_DH
NSA_KV = NSA_GROUPS * NSA_DH
CMP_BLOCK = 32
CMP_STRIDE = 16
CMP_HIDDEN = 256
SEL_BLOCK = 64
SEL_TOP = 16
WINDOW = 512

LANES = 128
ROW_TILE = 512
GLA_SEQ_TILE = 256
ATT_TILE = 256
VMEM_LIMIT = 56 * 1024 * 1024


def _dot(a, b):
    return jnp.dot(a, b, preferred_element_type=F32)


def _dot_nt(a, b):
    return lax.dot_general(a, b, (((1,), (1,)), ((), ())), preferred_element_type=F32)


def _rms(x, g):
    return x * lax.rsqrt(jnp.mean(x * x, axis=-1, keepdims=True) + NORM_EPS) * g


def _silu(z):
    return z * jax.nn.sigmoid(z)


def _params(*sem):
    return pltpu.CompilerParams(dimension_semantics=sem, vmem_limit_bytes=VMEM_LIMIT)


def _gla_proj_kernel(x_ref, g_ref, wm_ref, wg_ref, wup_ref, bgk_ref,
                     q_ref, k_ref, v_ref, gk_ref, zs_ref):
    h = _rms(x_ref[...], g_ref[...]).astype(BF16)
    q_ref[...] = _dot(h, wm_ref[:, 0:GLA_QK])
    k_ref[...] = _dot(h, wm_ref[:, GLA_QK:2 * GLA_QK])
    v_ref[...] = _dot(h, wm_ref[:, 2 * GLA_QK:2 * GLA_QK + GLA_V]).astype(BF16)
    zs_ref[...] = _silu(_dot(h, wm_ref[:, 2 * GLA_QK + GLA_V:])).astype(BF16)
    glr = _dot(h, wg_ref[...])
    pre = _dot(glr.astype(BF16), wup_ref[...]) + bgk_ref[...]
    gk_ref[...] = (jnp.minimum(pre, 0.0) - jnp.log(1.0 + jnp.exp(-jnp.abs(pre)))) / GLA_GATE_NORMALIZER


def _gla_proj(x2, pre_g, w_in, w_gk_up, b_gk):
    T = x2.shape[0]
    o_gl = 2 * GLA_QK + GLA_V
    wm = jnp.concatenate([w_in[:, :o_gl], w_in[:, o_gl + GLA_GATE_RANK:]], axis=1).astype(BF16)
    wg = jnp.pad(w_in[:, o_gl:o_gl + GLA_GATE_RANK], ((0, 0), (0, LANES - GLA_GATE_RANK))).astype(BF16)
    wup = jnp.pad(w_gk_up, ((0, LANES - GLA_GATE_RANK), (0, 0))).astype(BF16)
    row = lambda w: pl.BlockSpec((ROW_TILE, w), lambda i: (i, 0))
    full = lambda a: pl.BlockSpec(a.shape, lambda i: (0,) * a.ndim)
    args = (x2, pre_g.reshape(1, D_MODEL), wm, wg, wup, b_gk.reshape(1, GLA_QK))
    return pl.pallas_call(
        _gla_proj_kernel,
        grid=(T // ROW_TILE,),
        in_specs=[row(D_MODEL)] + [full(a) for a in args[1:]],
        out_specs=[row(GLA_QK), row(GLA_QK), row(GLA_V), row(GLA_QK), row(GLA_V)],
        out_shape=[jax.ShapeDtypeStruct((T, GLA_QK), F32), jax.ShapeDtypeStruct((T, GLA_QK), F32),
                   jax.ShapeDtypeStruct((T, GLA_V), BF16), jax.ShapeDtypeStruct((T, GLA_QK), F32),
                   jax.ShapeDtypeStruct((T, GLA_V), BF16)],
        compiler_params=_params("parallel"),
        name="gla_proj",
    )(*args)


def _gla_core_kernel(q_ref, k_ref, v_ref, gk_ref, zs_ref, x_ref, hn_ref, pn_ref, wo_ref,
                     o_ref, state_ref, og_ref):
    C = GLA_CHUNK

    @pl.when(pl.program_id(1) == 0)
    def _():
        state_ref[...] = jnp.zeros_like(state_ref)

    r = lax.broadcasted_iota(jnp.int32, (C, C), 0)
    c = lax.broadcasted_iota(jnp.int32, (C, C), 1)
    tril = r >= c
    lmat = jnp.where(tril, 1.0, 0.0).astype(BF16)
    scale = GLA_DK ** -0.5
    hn = hn_ref[...]
    for ci in range(GLA_SEQ_TILE // C):
        rows = slice(ci * C, (ci + 1) * C)
        g = gk_ref[0, rows, :]
        g_hi = g.astype(BF16)
        g_lo = (g - g_hi.astype(F32)).astype(BF16)
        b_all = _dot(lmat, g_hi) + _dot(lmat, g_lo)
        for h in range(GLA_HEADS):
            ksl = slice(h * GLA_DK, (h + 1) * GLA_DK)
            vsl = slice(h * GLA_DV, (h + 1) * GLA_DV)
            b = b_all[:, ksl]
            b_last = b[C - 1:C, :]
            eb = jnp.exp(b)
            kh = k_ref[0, rows, ksl]
            qe = (q_ref[0, rows, ksl] * scale * eb).astype(BF16)
            ke = (kh * jnp.exp(-b)).astype(BF16)
            kd = kh * jnp.exp(b_last - b)
            a = jnp.where(tril, _dot_nt(qe, ke), 0.0).astype(BF16)
            vh = v_ref[0, rows, vsl]
            st = state_ref[h]
            o = _dot(a, vh) + _dot(qe, st.astype(BF16))
            kd_t = jnp.transpose(kd).astype(BF16)
            dec = jnp.transpose(eb)[:, C - 1:C]
            state_ref[h] = dec * st + _dot(kd_t, vh)
            og_ref[rows, vsl] = _rms(o, hn)
    u = (og_ref[...] * zs_ref[0].astype(F32)).astype(BF16)
    y = _dot(u, wo_ref[...])
    o_ref[0] = x_ref[0] + _rms(y, pn_ref[...])


def _gla_core(q, k, v, gk, zs, x, head_norm, post_g, w_out):
    B, S, _ = x.shape
    ts = GLA_SEQ_TILE
    blk = lambda w: pl.BlockSpec((1, ts, w), lambda b, s: (b, s, 0))
    full = lambda a: pl.BlockSpec(a.shape, lambda b, s: (0,) * a.ndim)
    args = (q, k, v, gk, zs, x, head_norm.reshape(1, GLA_DV), post_g.reshape(1, D_MODEL), w_out.astype(BF16))
    return pl.pallas_call(
        _gla_core_kernel,
        grid=(B, S // ts),
        in_specs=[blk(GLA_QK), blk(GLA_QK), blk(GLA_V), blk(GLA_QK), blk(GLA_V), blk(D_MODEL)]
        + [full(a) for a in args[6:]],
        out_specs=blk(D_MODEL),
        out_shape=jax.ShapeDtypeStruct((B, S, D_MODEL), F32),
        scratch_shapes=[pltpu.VMEM((GLA_HEADS, GLA_DK, GLA_DV), F32), pltpu.VMEM((ts, GLA_V), F32)],
        compiler_params=_params("parallel", "arbitrary"),
        name="gla_core",
    )(*args)


def _nsa_proj_kernel(x_ref, pos_ref, g_ref, inv_ref, sgn_ref, wm_ref, wg_ref, bg_ref,
                     q_ref, kc_ref, vc_ref, ks_ref, vs_ref, kw_ref, vw_ref, gt_ref, zs_ref):
    h = _rms(x_ref[...], g_ref[...]).astype(BF16)
    ang = pos_ref[...].astype(F32) * inv_ref[...]
    cos = jnp.cos(ang)
    sin = jnp.sin(ang) * sgn_ref[...]
    first_half = sgn_ref[...] < 0.0

    def rope(y):
        partner = jnp.where(first_half, pltpu.roll(y, LANES - NSA_DH // 2, axis=1),
                            pltpu.roll(y, NSA_DH // 2, axis=1))
        return y * cos + partner * sin

    def seg(off, width):
        return _dot(h, wm_ref[:, off:off + width])

    q_scale = NSA_DH ** -0.5 * LOG2_E
    for j in range(NSA_Q // LANES):
        q_ref[:, j * LANES:(j + 1) * LANES] = (rope(seg(j * LANES, LANES)) * q_scale).astype(BF16)
    off = NSA_Q
    for ref, roped in ((kc_ref, True), (vc_ref, False), (ks_ref, True), (vs_ref, False),
                       (kw_ref, True), (vw_ref, False)):
        for j in range(NSA_KV // LANES):
            y = seg(off + j * LANES, LANES)
            if roped:
                y = rope(y)
            ref[:, j * LANES:(j + 1) * LANES] = y.astype(ref.dtype)
        off += NSA_KV
    zs_ref[...] = _silu(seg(off, NSA_Q)).astype(BF16)
    gt_ref[...] = jax.nn.sigmoid(_dot(h, wg_ref[...]) + bg_ref[...])


def _nsa_proj(x2, pos2, pre_g, w_in, b_gate):
    T = x2.shape[0]
    o_gl = NSA_Q + 6 * NSA_KV
    n_gl = 3 * NSA_HEADS
    wm = jnp.concatenate([w_in[:, :o_gl], w_in[:, o_gl + n_gl:]], axis=1).astype(BF16)
    wg = jnp.pad(w_in[:, o_gl:o_gl + n_gl], ((0, 0), (0, LANES - n_gl))).astype(BF16)
    bg = jnp.pad(b_gate, (0, LANES - n_gl)).reshape(1, LANES)
    half = NSA_DH // 2
    inv = ROPE_THETA ** (-jnp.arange(0, NSA_DH, 2, dtype=F32) / NSA_DH)
    inv = jnp.tile(inv, LANES // half).reshape(1, LANES)
    sgn = jnp.where((jnp.arange(LANES) % NSA_DH) < half, -1.0, 1.0).astype(F32).reshape(1, LANES)
    row = lambda w: pl.BlockSpec((ROW_TILE, w), lambda i: (i, 0))
    full = lambda a: pl.BlockSpec(a.shape, lambda i: (0,) * a.ndim)
    args = (x2, pos2, pre_g.reshape(1, D_MODEL), inv, sgn, wm, wg, bg)
    sds = jax.ShapeDtypeStruct
    return pl.pallas_call(
        _nsa_proj_kernel,
        grid=(T // ROW_TILE,),
        in_specs=[row(D_MODEL), row(1)] + [full(a) for a in args[2:]],
        out_specs=[row(NSA_Q)] + [row(NSA_KV)] * 6 + [row(LANES), row(NSA_Q)],
        out_shape=[sds((T, NSA_Q), BF16), sds((T, NSA_KV), F32), sds((T, NSA_KV), F32),
                   sds((T, NSA_KV), BF16), sds((T, NSA_KV), BF16), sds((T, NSA_KV), BF16),
                   sds((T, NSA_KV), BF16), sds((T, LANES), F32), sds((T, NSA_Q), BF16)],
        compiler_params=_params("parallel"),
        name="nsa_proj",
    )(*args)


def _compress_kernel(p_ref, pe_ref, w1_ref, w2_ref, o_ref):
    half = CMP_STRIDE * NSA_DH
    p = p_ref[0, 0].astype(BF16)
    a = _dot(p, w1_ref[0:half, :])
    b = _dot(p, w1_ref[half:, :])
    bias = _dot(pe_ref[...].astype(BF16), w1_ref[...])[0:1]
    n = p.shape[0]
    hid = a + pltpu.roll(b, n - 1, axis=0) + bias
    o_ref[0, 0] = _dot(_silu(hid).astype(BF16), w2_ref[...])


def _compress(t, pe, w1, w2):
    B, S, _ = t.shape
    n = S // CMP_STRIDE
    pieces = t.reshape(B, n, CMP_STRIDE, NSA_GROUPS, NSA_DH).transpose(0, 3, 1, 2, 4)
    pieces = pieces.reshape(B, NSA_GROUPS, n, CMP_STRIDE * NSA_DH)
    pe_flat = jnp.pad(pe.reshape(1, CMP_BLOCK * NSA_DH), ((0, 7), (0, 0)))
    full = lambda a: pl.BlockSpec(a.shape, lambda b, g: (0,) * a.ndim)
    args = (pieces, pe_flat, w1.astype(BF16), w2.astype(BF16))
    return pl.pallas_call(
        _compress_kernel,
        grid=(B, NSA_GROUPS),
        in_specs=[pl.BlockSpec((1, 1, n, CMP_STRIDE * NSA_DH), lambda b, g: (b, g, 0, 0))]
        + [full(a) for a in args[1:]],
        out_specs=pl.BlockSpec((1, 1, n, NSA_DH), lambda b, g: (b, g, 0, 0)),
        out_shape=jax.ShapeDtypeStruct((B, NSA_GROUPS, n, NSA_DH), F32),
        compiler_params=_params("parallel", "parallel"),
        name="nsa_compress",
    )(*args)


V_ROWS = NSA_DH + 16


def _nsa_attn_kernel(q_ref, kcm_ref, vct_ref, kse_ref, vst_ref, kwn_ref, vwt_ref, gt_ref, ovt_ref,
                     o_ref, m_ref, acc_ref, s0_ref, s1_ref, ms_ref, sw0_ref, sw1_ref, sw2_ref, mw_ref,
                     *, n_cmp, n_blk):
    tq = ATT_TILE
    qi = pl.program_id(2)
    t0 = qi * tq
    t_row = t0 + lax.broadcasted_iota(jnp.int32, (1, tq), 1)
    q_t = jnp.transpose(q_ref[0].astype(F32)).astype(BF16)
    q_heads = [q_t[h * NSA_DH:(h + 1) * NSA_DH] for h in range(NSA_HPG)]

    def scores(buf, k_tile, ws, bias):
        s_ref, mt_ref = buf
        for h, w in enumerate(ws):
            s = _dot(k_tile, w)
            if bias is not None:
                s = s + bias
            s_ref[h] = s
            mt_ref[h] = jnp.max(s, axis=0, keepdims=True)

    def absorb(br, buf, vt_tile):
        s_ref, mt_ref = buf
        for h in range(NSA_HPG):
            m_old = m_ref[br, h]
            m_new = jnp.maximum(m_old, mt_ref[h])
            p = jnp.exp2(s_ref[h] - m_new).astype(BF16)
            acc_ref[br, h] = jnp.exp2(m_old - m_new) * acc_ref[br, h] + _dot(vt_tile, p)
            m_ref[br, h] = m_new

    def result(br, h):
        acc = acc_ref[br, h]
        return acc[0:NSA_DH] / acc[NSA_DH:NSA_DH + 1]

    m_ref[...] = jnp.full(m_ref.shape, NEG, F32)
    acc_ref[...] = jnp.zeros(acc_ref.shape, F32)
    r_idx = lax.broadcasted_iota(jnp.int32, (tq, tq), 0)
    c_full = lax.broadcasted_iota(jnp.int32, (tq, tq), 1)
    diag_bias = jnp.where(r_idx <= c_full, 0.0, NEG)
    far_bias = jnp.where((r_idx > c_full) & (qi >= 2), 0.0, NEG)
    mid_bias = jnp.where(jnp.broadcast_to(qi >= 1, (tq, tq)), 0.0, NEG)

    kcm = kcm_ref[0, 0]
    s_cmp = [_dot(kcm, q_heads[h]) for h in range(NSA_HPG)]
    w_bufs = ((sw0_ref, mw_ref.at[0]), (sw1_ref, mw_ref.at[1]), (sw2_ref, mw_ref.at[2]))
    w_tiles = (jnp.maximum(qi - 2, 0), jnp.maximum(qi - 1, 0), qi)
    for buf, kt, bias in zip(w_bufs, w_tiles, (far_bias, mid_bias, diag_bias)):
        scores(buf, kwn_ref[0, 0, kt], q_heads, bias)

    c_idx = lax.broadcasted_iota(jnp.int32, (n_cmp, 1), 0)
    cvalid = (c_idx * CMP_STRIDE + (CMP_BLOCK - 1)) <= t_row
    anyvalid = jnp.where(t_row >= CMP_BLOCK - 1, 1.0, 0.0)
    vct = vct_ref[0, 0]
    psum = jnp.zeros((n_cmp, tq), F32)
    o_cmp = []
    for h in range(NSA_HPG):
        s = jnp.where(cvalid, s_cmp[h], NEG)
        e = jnp.exp2(s - jnp.max(s, axis=0, keepdims=True))
        p = e * (anyvalid / jnp.sum(e, axis=0, keepdims=True))
        psum = psum + p
        o_cmp.append(_dot(vct, p.astype(BF16)))
    p_hi = psum.astype(BF16)
    p_lo = (psum - p_hi.astype(F32)).astype(BF16)
    ovt = ovt_ref[...]
    imp = _dot(ovt, p_hi) + _dot(ovt, p_lo)

    j_full = lax.broadcasted_iota(jnp.int32, (n_blk, tq), 0)
    cur = t_row // SEL_BLOCK
    forced = (j_full == 0) | (j_full == cur) | (j_full == cur - 1)
    causal_blk = j_full * SEL_BLOCK <= t_row
    val = jnp.where(forced, BIG, jnp.where(causal_blk, imp, NEG))
    sub = 8
    groups = [val[r:r + sub] for r in range(0, n_blk, sub)]
    j_grp = lax.broadcasted_iota(jnp.int32, (sub, tq), 0)
    ranks = [jnp.zeros((sub, tq), F32) for _ in groups]
    for k in range(n_blk):
        vk = val[k:k + 1, :]
        for gi, vg in enumerate(groups):
            lo = gi * sub
            ge = jnp.where(vk >= vg, 1.0, 0.0)
            gt = jnp.where(vk > vg, 1.0, 0.0)
            if lo > k:
                inc = ge
            elif lo + sub - 1 <= k:
                inc = gt
            else:
                inc = jnp.where(j_grp + lo > k, ge, gt)
            ranks[gi] = ranks[gi] + inc
    rank = jnp.concatenate(ranks, axis=0)
    sel_bias = jnp.where(rank < float(SEL_TOP), 0.0, NEG).astype(BF16)

    w_sel = [jnp.concatenate([q_heads[h], sel_bias], axis=0) for h in range(NSA_HPG)]
    buf0 = (s0_ref, ms_ref.at[0])
    buf1 = (s1_ref, ms_ref.at[1])
    SEL, WIN = 0, 1
    scores(buf0, kse_ref[0, 0, qi], w_sel, diag_bias)

    for buf, kt in zip(w_bufs, w_tiles):
        absorb(WIN, buf, vwt_ref[0, 0, kt])

    def seq_tile(j):
        return jnp.where(j == 0, qi, j - 1)

    def pair(i, carry):
        scores(buf1, kse_ref[0, 0, 2 * i], w_sel, None)
        absorb(SEL, buf0, vst_ref[0, 0, seq_tile(2 * i)])
        scores(buf0, kse_ref[0, 0, 2 * i + 1], w_sel, None)
        absorb(SEL, buf1, vst_ref[0, 0, 2 * i])
        return carry

    lax.fori_loop(0, qi // 2, pair, 0)

    @pl.when(qi % 2 == 0)
    def _():
        absorb(SEL, buf0, vst_ref[0, 0, seq_tile(qi)])

    @pl.when(qi % 2 == 1)
    def _():
        scores(buf1, kse_ref[0, 0, qi - 1], w_sel, None)
        absorb(SEL, buf0, vst_ref[0, 0, seq_tile(qi - 1)])
        absorb(SEL, buf1, vst_ref[0, 0, qi - 1])

    gates = gt_ref[0, 0]
    outs = []
    for h in range(NSA_HPG):
        outs.append(gates[3 * h:3 * h + 1] * o_cmp[h] + gates[3 * h + 1:3 * h + 2] * result(SEL, h)
                    + gates[3 * h + 2:3 * h + 3] * result(WIN, h))
    o_ref[0] = jnp.transpose(jnp.concatenate(outs, axis=0)).astype(BF16)


def _nsa_attn(q, kcmp, vcmp, ks, vs, kw, vw, gates):
    B, S, _ = q.shape
    G, DH, tq = NSA_GROUPS, NSA_DH, ATT_TILE
    assert WINDOW == 2 * tq
    n_cmp = kcmp.shape[2]
    nt = S // tq
    ns = S // SEL_BLOCK

    def keys(k):
        return k.reshape(B, nt, tq, G, DH).transpose(0, 3, 1, 2, 4)

    def vals_t(v):
        vt = v.reshape(B, nt, tq, G, DH).transpose(0, 3, 1, 4, 2)
        return jnp.concatenate([vt, jnp.ones((B, G, nt, V_ROWS - DH, tq), BF16)], axis=3)

    onehot = (jnp.arange(S)[:, None] // SEL_BLOCK == jnp.arange(ns)[None, :]).astype(BF16)
    kse = jnp.concatenate([keys(ks), jnp.broadcast_to(onehot.reshape(nt, tq, ns), (B, G, nt, tq, ns))], axis=4)
    kwn = keys(kw)
    vst, vwt = vals_t(vs), vals_t(vw)
    kcm = kcmp.astype(BF16)
    vct = vcmp.transpose(0, 1, 3, 2).astype(BF16)
    n_g = 3 * NSA_HPG
    gt = gates[:, :, :3 * NSA_HEADS].reshape(B, S, G, n_g).transpose(0, 2, 3, 1)
    gt = jnp.pad(gt, ((0, 0), (0, 0), (0, 16 - n_g), (0, 0)))
    cs = jnp.arange(n_cmp)[None, :] * CMP_STRIDE
    ss = jnp.arange(ns)[:, None] * SEL_BLOCK
    n_valid = (S - CMP_BLOCK) // CMP_STRIDE + 1
    ovt = ((cs < ss + SEL_BLOCK) & (cs + CMP_BLOCK > ss) & (jnp.arange(n_cmp)[None, :] < n_valid)).astype(BF16)

    per_bg = lambda a: pl.BlockSpec((1, 1) + a.shape[2:], lambda b, g, i: (b, g) + (0,) * (a.ndim - 2))
    return pl.pallas_call(
        functools.partial(_nsa_attn_kernel, n_cmp=n_cmp, n_blk=ns),
        grid=(B, G, nt),
        in_specs=[pl.BlockSpec((1, tq, NSA_HPG * DH), lambda b, g, i: (b, i, g)),
                  per_bg(kcm), per_bg(vct), per_bg(kse), per_bg(vst), per_bg(kwn), per_bg(vwt),
                  pl.BlockSpec((1, 1, 16, tq), lambda b, g, i: (b, g, 0, i)),
                  pl.BlockSpec(ovt.shape, lambda b, g, i: (0, 0))],
        out_specs=pl.BlockSpec((1, tq, NSA_HPG * DH), lambda b, g, i: (b, i, g)),
        out_shape=jax.ShapeDtypeStruct((B, S, NSA_Q), BF16),
        scratch_shapes=[pltpu.VMEM((2, NSA_HPG, 1, tq), F32), pltpu.VMEM((2, NSA_HPG, V_ROWS, tq), F32)]
        + [pltpu.VMEM((NSA_HPG, tq, tq), F32)] * 2 + [pltpu.VMEM((2, NSA_HPG, 1, tq), F32)]
        + [pltpu.VMEM((NSA_HPG, tq, tq), F32)] * 3 + [pltpu.VMEM((3, NSA_HPG, 1, tq), F32)],
        compiler_params=_params("parallel", "parallel", "arbitrary"),
        name="nsa_attn",
    )(q, kcm, vct, kse, vst, kwn, vwt, gt, ovt)


def _nsa_out_kernel(og_ref, zs_ref, x_ref, pn_ref, wo_ref, o_ref):
    u = (og_ref[...].astype(F32) * zs_ref[...].astype(F32)).astype(BF16)
    o_ref[...] = x_ref[...] + _rms(_dot(u, wo_ref[...]), pn_ref[...])


def _nsa_out(og, zs, x2, post_g, w_out):
    T = x2.shape[0]
    row = lambda w: pl.BlockSpec((ROW_TILE, w), lambda i: (i, 0))
    full = lambda a: pl.BlockSpec(a.shape, lambda i: (0,) * a.ndim)
    args = (og, zs, x2, post_g.reshape(1, D_MODEL), w_out.astype(BF16))
    return pl.pallas_call(
        _nsa_out_kernel,
        grid=(T // ROW_TILE,),
        in_specs=[row(NSA_Q), row(NSA_Q), row(D_MODEL)] + [full(a) for a in args[3:]],
        out_specs=row(D_MODEL),
        out_shape=jax.ShapeDtypeStruct((T, D_MODEL), F32),
        compiler_params=_params("parallel"),
        name="nsa_out",
    )(*args)


def _gla_layer(x, pre_g, post_g, w_in, w_gk_up, b_gk, head_norm, w_out):
    B, S, D = x.shape
    q, k, v, gk, zs = _gla_proj(x.reshape(B * S, D), pre_g, w_in, w_gk_up, b_gk)
    r3 = lambda a: a.reshape(B, S, a.shape[-1])
    return _gla_core(r3(q), r3(k), r3(v), r3(gk), r3(zs), x, head_norm, post_g, w_out)


def _nsa_layer(x, positions, pre_g, post_g, w_in, b_gate, pe_k, pe_v, ck_w1, ck_w2, cv_w1, cv_w2, w_out):
    B, S, D = x.shape
    x2 = x.reshape(B * S, D)
    q, kc, vc, ks, vs, kw, vw, gates, zs = _nsa_proj(x2, positions.reshape(B * S, 1), pre_g, w_in, b_gate)
    r3 = lambda a: a.reshape(B, S, a.shape[-1])
    kcmp = _compress(r3(kc), pe_k, ck_w1, ck_w2)
    vcmp = _compress(r3(vc), pe_v, cv_w1, cv_w2)
    og = _nsa_attn(r3(q), kcmp, vcmp, r3(ks), r3(vs), r3(kw), r3(vw), r3(gates))
    return _nsa_out(og.reshape(B * S, NSA_Q), zs, x2, post_g, w_out).reshape(B, S, D)


def kernel(x, positions, pre_norm, post_norm, gla_w_in, gla_w_gk_up, gla_b_gk, gla_head_norm, gla_w_out,
           nsa_w_in, nsa_b_gate, nsa_pe_k, nsa_pe_v, nsa_ck_w1, nsa_ck_w2, nsa_cv_w1, nsa_cv_w2, nsa_w_out):
    depth = pre_norm.shape[0]
    for i in range(depth):
        j = i // 2
        if i % 2 == 0:
            x = _gla_layer(x, pre_norm[i], post_norm[i], gla_w_in[j], gla_w_gk_up[j], gla_b_gk[j],
                           gla_head_norm[j], gla_w_out[j])
        else:
            x = _nsa_layer(x, positions, pre_norm[i], post_norm[i], nsa_w_in[j], nsa_b_gate[j],
                           nsa_pe_k[j], nsa_pe_v[j], nsa_ck_w1[j], nsa_ck_w2[j], nsa_cv_w1[j],
                           nsa_cv_w2[j], nsa_w_out[j])
    return x
```

```python
import functools

import jax
import jax.numpy as jnp
from jax import lax
from jax.experimental import pallas as pl
from jax.experimental.pallas import tpu as pltpu

F32 = jnp.float32
BF16 = jnp.bfloat16

D_MODEL = 1024
NORM_EPS = 1e-6
ROPE_THETA = 10000.0
NEG = -1e30
BIG = 1e9
LOG2_E = 1.4426950408889634

GLA_HEADS = 4
GLA_DK = 128
GLA_DV = 256
GLA_QK = GLA_HEADS * GLA_DK
GLA_V = GLA_HEADS * GLA_DV
GLA_GATE_RANK = 16
GLA_GATE_NORMALIZER = 16.0
GLA_CHUNK = 64

NSA_HEADS = 16
NSA_GROUPS = 4
NSA_HPG = NSA_HEADS // NSA_GROUPS
NSA_DH = 64
NSA_Q = NSA_HEADS * NSA_DH
NSA_KV = NSA_GROUPS * NSA_DH
CMP_BLOCK = 32
CMP_STRIDE = 16
CMP_HIDDEN = 256
SEL_BLOCK = 64
SEL_TOP = 16
WINDOW = 512

LANES = 128
ROW_TILE = 512
GLA_SEQ_TILE = 256
ATT_TILE = 256
VMEM_LIMIT = 56 * 1024 * 1024


def _dot(a, b):
    return jnp.dot(a, b, preferred_element_type=F32)


def _dot_nt(a, b):
    return lax.dot_general(a, b, (((1,), (1,)), ((), ())), preferred_element_type=F32)


def _rms(x, g):
    return x * lax.rsqrt(jnp.mean(x * x, axis=-1, keepdims=True) + NORM_EPS) * g


def _silu(z):
    return z * jax.nn.sigmoid(z)


def _params(*sem):
    return pltpu.CompilerParams(dimension_semantics=sem, vmem_limit_bytes=VMEM_LIMIT)


def _gla_proj_kernel(x_ref, g_ref, wm_ref, wg_ref, wup_ref, bgk_ref,
                     q_ref, k_ref, v_ref, gk_ref, zs_ref):
    h = _rms(x_ref[...], g_ref[...]).astype(BF16)
    q_ref[...] = _dot(h, wm_ref[:, 0:GLA_QK])
    k_ref[...] = _dot(h, wm_ref[:, GLA_QK:2 * GLA_QK])
    v_ref[...] = _dot(h, wm_ref[:, 2 * GLA_QK:2 * GLA_QK + GLA_V]).astype(BF16)
    zs_ref[...] = _silu(_dot(h, wm_ref[:, 2 * GLA_QK + GLA_V:])).astype(BF16)
    glr = _dot(h, wg_ref[...])
    pre = _dot(glr.astype(BF16), wup_ref[...]) + bgk_ref[...]
    gk_ref[...] = (jnp.minimum(pre, 0.0) - jnp.log(1.0 + jnp.exp(-jnp.abs(pre)))) / GLA_GATE_NORMALIZER


def _gla_proj(x2, pre_g, w_in, w_gk_up, b_gk):
    T = x2.shape[0]
    o_gl = 2 * GLA_QK + GLA_V
    wm = jnp.concatenate([w_in[:, :o_gl], w_in[:, o_gl + GLA_GATE_RANK:]], axis=1).astype(BF16)
    wg = jnp.pad(w_in[:, o_gl:o_gl + GLA_GATE_RANK], ((0, 0), (0, LANES - GLA_GATE_RANK))).astype(BF16)
    wup = jnp.pad(w_gk_up, ((0, LANES - GLA_GATE_RANK), (0, 0))).astype(BF16)
    row = lambda w: pl.BlockSpec((ROW_TILE, w), lambda i: (i, 0))
    full = lambda a: pl.BlockSpec(a.shape, lambda i: (0,) * a.ndim)
    args = (x2, pre_g.reshape(1, D_MODEL), wm, wg, wup, b_gk.reshape(1, GLA_QK))
    return pl.pallas_call(
        _gla_proj_kernel,
        grid=(T // ROW_TILE,),
        in_specs=[row(D_MODEL)] + [full(a) for a in args[1:]],
        out_specs=[row(GLA_QK), row(GLA_QK), row(GLA_V), row(GLA_QK), row(GLA_V)],
        out_shape=[jax.ShapeDtypeStruct((T, GLA_QK), F32), jax.ShapeDtypeStruct((T, GLA_QK), F32),
                   jax.ShapeDtypeStruct((T, GLA_V), BF16), jax.ShapeDtypeStruct((T, GLA_QK), F32),
                   jax.ShapeDtypeStruct((T, GLA_V), BF16)],
        compiler_params=_params("parallel"),
        name="gla_proj",
    )(*args)


def _gla_core_kernel(q_ref, k_ref, v_ref, gk_ref, zs_ref, x_ref, hn_ref, pn_ref, wo_ref,
                     o_ref, state_ref, og_ref):
    C = GLA_CHUNK

    @pl.when(pl.program_id(1) == 0)
    def _():
        state_ref[...] = jnp.zeros_like(state_ref)

    ts = GLA_SEQ_TILE
    r = lax.broadcasted_iota(jnp.int32, (C, C), 0)
    c = lax.broadcasted_iota(jnp.int32, (C, C), 1)
    tril = r >= c
    rt = lax.broadcasted_iota(jnp.int32, (ts, ts), 0)
    ct = lax.broadcasted_iota(jnp.int32, (ts, ts), 1)
    lmat = jnp.where((rt >= ct) & (rt // C == ct // C), 1.0, 0.0).astype(BF16)
    scale = GLA_DK ** -0.5
    hn = hn_ref[...]
    g = gk_ref[0]
    g_hi = g.astype(BF16)
    g_lo = (g - g_hi.astype(F32)).astype(BF16)
    b_all = _dot(lmat, g_hi) + _dot(lmat, g_lo)

    units = []
    for ci in range(ts // C):
        rows = slice(ci * C, (ci + 1) * C)
        for h in range(GLA_HEADS):
            ksl = slice(h * GLA_DK, (h + 1) * GLA_DK)
            vsl = slice(h * GLA_DV, (h + 1) * GLA_DV)
            b = b_all[rows, ksl]
            b_last = b[C - 1:C, :]
            eb = jnp.exp(b)
            kh = k_ref[0, rows, ksl]
            qe = (q_ref[0, rows, ksl] * scale * eb).astype(BF16)
            ke = (kh * jnp.exp(-b)).astype(BF16)
            kd = kh * jnp.exp(b_last - b)
            a = jnp.where(tril, _dot_nt(qe, ke), 0.0).astype(BF16)
            vh = v_ref[0, rows, vsl]
            o_intra = _dot(a, vh)
            kv = _dot(jnp.transpose(kd).astype(BF16), vh)
            dec = jnp.transpose(eb)[:, C - 1:C]
            units.append((rows, vsl, h, qe, o_intra, kv, dec))

    state = [state_ref[h] for h in range(GLA_HEADS)]
    for rows, vsl, h, qe, o_intra, kv, dec in units:
        o = o_intra + _dot(qe, state[h].astype(BF16))
        state[h] = dec * state[h] + kv
        og_ref[rows, vsl] = _rms(o, hn)
    for h in range(GLA_HEADS):
        state_ref[h] = state[h]
    u = (og_ref[...] * zs_ref[0].astype(F32)).astype(BF16)
    y = _dot(u, wo_ref[...])
    o_ref[0] = x_ref[0] + _rms(y, pn_ref[...])


def _gla_core(q, k, v, gk, zs, x, head_norm, post_g, w_out):
    B, S, _ = x.shape
    ts = GLA_SEQ_TILE
    blk = lambda w: pl.BlockSpec((1, ts, w), lambda b, s: (b, s, 0))
    full = lambda a: pl.BlockSpec(a.shape, lambda b, s: (0,) * a.ndim)
    args = (q, k, v, gk, zs, x, head_norm.reshape(1, GLA_DV), post_g.reshape(1, D_MODEL), w_out.astype(BF16))
    return pl.pallas_call(
        _gla_core_kernel,
        grid=(B, S // ts),
        in_specs=[blk(GLA_QK), blk(GLA_QK), blk(GLA_V), blk(GLA_QK), blk(GLA_V), blk(D_MODEL)]
        + [full(a) for a in args[6:]],
        out_specs=blk(D_MODEL),
        out_shape=jax.ShapeDtypeStruct((B, S, D_MODEL), F32),
        scratch_shapes=[pltpu.VMEM((GLA_HEADS, GLA_DK, GLA_DV), F32), pltpu.VMEM((ts, GLA_V), F32)],
        compiler_params=_params("parallel", "arbitrary"),
        name="gla_core",
    )(*args)


V_ROWS = NSA_DH + 16
SEG = 256


def _nsa_proj_kernel(x_ref, pos_ref, g_ref, inv_ref, sgn_ref, wm_ref, wg_ref, bg_ref,
                     q_ref, kc_ref, vc_ref, kse_ref, kwn_ref, vst_ref, vwt_ref, gt_ref, zs_ref):
    rows = x_ref.shape[1]
    tiles = rows // ATT_TILE
    h = _rms(x_ref[0], g_ref[...]).astype(BF16)

    def seg(off):
        return _dot(h, wm_ref[:, off:off + SEG])

    o_kc, o_vc, o_ks, o_vs, o_kw, o_vw, o_z = (NSA_Q + i * NSA_KV for i in range(7))

    for j in range(NSA_Q // SEG):
        zs_ref[0, :, j * SEG:(j + 1) * SEG] = _silu(seg(o_z + j * SEG)).astype(BF16)
    vc_ref[0] = seg(o_vc)
    ones = jnp.ones((V_ROWS - NSA_DH, ATT_TILE), BF16)
    for ref, off in ((vst_ref, o_vs), (vwt_ref, o_vw)):
        y = seg(off)
        for pair in range(NSA_GROUPS // 2):
            y_t = jnp.transpose(y[:, pair * LANES:(pair + 1) * LANES])
            for gg in range(2):
                for tl in range(tiles):
                    blk = y_t[gg * NSA_DH:(gg + 1) * NSA_DH, tl * ATT_TILE:(tl + 1) * ATT_TILE]
                    ref[0, 2 * pair + gg, tl, 0:NSA_DH, :] = blk.astype(BF16)
                    ref[0, 2 * pair + gg, tl, NSA_DH:V_ROWS, :] = ones
    gates_t = jnp.transpose(jax.nn.sigmoid(_dot(h, wg_ref[...]) + bg_ref[...]))
    for g in range(NSA_GROUPS):
        gt_ref[0, g] = gates_t[16 * g:16 * (g + 1), :]

    ang = pos_ref[0].astype(F32) * inv_ref[...]
    cos = jnp.cos(ang)
    sin = jnp.sin(ang) * sgn_ref[...]
    first_half = sgn_ref[...] < 0.0

    def rope(y):
        partner = jnp.where(first_half, pltpu.roll(y, LANES - NSA_DH // 2, axis=1),
                            pltpu.roll(y, NSA_DH // 2, axis=1))
        return y * cos + partner * sin

    q_scale = NSA_DH ** -0.5 * LOG2_E
    for j in range(NSA_Q // SEG):
        y = seg(j * SEG)
        for s in range(SEG // LANES):
            lanes = slice(j * SEG + s * LANES, j * SEG + (s + 1) * LANES)
            q_ref[0, :, lanes] = (rope(y[:, s * LANES:(s + 1) * LANES]) * q_scale).astype(BF16)
    y = seg(o_kc)
    for s in range(NSA_KV // LANES):
        kc_ref[0, :, s * LANES:(s + 1) * LANES] = rope(y[:, s * LANES:(s + 1) * LANES])

    lane = lax.broadcasted_iota(jnp.int32, (rows, LANES), 1)
    key_blk = (pl.program_id(1) * rows + lax.broadcasted_iota(jnp.int32, (rows, LANES), 0)) // SEL_BLOCK
    onehot = jnp.where(lane - NSA_DH == key_blk, 1.0, 0.0)
    low = lane < NSA_DH
    for ref, off, fill in ((kse_ref, o_ks, onehot), (kwn_ref, o_kw, 0.0)):
        y = seg(off)
        for pair in range(NSA_GROUPS // 2):
            slab = rope(y[:, pair * LANES:(pair + 1) * LANES])
            for gg, src in enumerate((slab, pltpu.roll(slab, NSA_DH, axis=1))):
                ext = jnp.where(low, src, fill).astype(BF16)
                for tl in range(tiles):
                    ref[0, 2 * pair + gg, tl] = ext[tl * ATT_TILE:(tl + 1) * ATT_TILE]


def _nsa_proj(x, positions, pre_g, w_in, b_gate):
    B, S, _ = x.shape
    G, rows = NSA_GROUPS, ROW_TILE
    nt, tiles = S // ATT_TILE, ROW_TILE // ATT_TILE
    o_gl = NSA_Q + 6 * NSA_KV
    n_gl = 3 * NSA_HEADS
    wm = jnp.concatenate([w_in[:, :o_gl], w_in[:, o_gl + n_gl:]], axis=1).astype(BF16)

    def per_group(a):
        a = a.reshape(a.shape[:-1] + (G, 3 * NSA_HPG))
        a = jnp.pad(a, [(0, 0)] * (a.ndim - 1) + [(0, 16 - 3 * NSA_HPG)])
        a = a.reshape(a.shape[:-2] + (16 * G,))
        return jnp.pad(a, [(0, 0)] * (a.ndim - 1) + [(0, LANES - 16 * G)])

    wg = per_group(w_in[:, o_gl:o_gl + n_gl]).astype(BF16)
    bg = per_group(b_gate).reshape(1, LANES)
    half = NSA_DH // 2
    inv = ROPE_THETA ** (-jnp.arange(0, NSA_DH, 2, dtype=F32) / NSA_DH)
    inv = jnp.tile(inv, LANES // half).reshape(1, LANES)
    sgn = jnp.where((jnp.arange(LANES) % NSA_DH) < half, -1.0, 1.0).astype(F32).reshape(1, LANES)
    row = lambda w: pl.BlockSpec((1, rows, w), lambda b, j: (b, j, 0))
    full = lambda a: pl.BlockSpec(a.shape, lambda b, j: (0,) * a.ndim)
    keys = pl.BlockSpec((1, G, tiles, ATT_TILE, LANES), lambda b, j: (b, 0, j, 0, 0))
    vals = pl.BlockSpec((1, G, tiles, V_ROWS, ATT_TILE), lambda b, j: (b, 0, j, 0, 0))
    args = (x, positions.reshape(B, S, 1), pre_g.reshape(1, D_MODEL), inv, sgn, wm, wg, bg)
    sds = jax.ShapeDtypeStruct
    return pl.pallas_call(
        _nsa_proj_kernel,
        grid=(B, S // rows),
        in_specs=[row(D_MODEL), row(1)] + [full(a) for a in args[2:]],
        out_specs=[row(NSA_Q), row(NSA_KV), row(NSA_KV), keys, keys, vals, vals,
                   pl.BlockSpec((1, G, 16, rows), lambda b, j: (b, 0, 0, j)), row(NSA_Q)],
        out_shape=[sds((B, S, NSA_Q), BF16), sds((B, S, NSA_KV), F32), sds((B, S, NSA_KV), F32),
                   sds((B, G, nt, ATT_TILE, LANES), BF16), sds((B, G, nt, ATT_TILE, LANES), BF16),
                   sds((B, G, nt, V_ROWS, ATT_TILE), BF16), sds((B, G, nt, V_ROWS, ATT_TILE), BF16),
                   sds((B, G, 16, S), F32), sds((B, S, NSA_Q), BF16)],
        compiler_params=_params("parallel", "parallel"),
        name="nsa_proj",
    )(*args)


def _compress_kernel(p_ref, pe_ref, w1_ref, wj_ref, w2_ref, o_ref):
    n = p_ref.shape[1]
    width = NSA_GROUPS * NSA_DH
    bias = _dot(pe_ref[...].astype(BF16), w1_ref[...])[0:1]
    w2 = w2_ref[...]
    acc = [jnp.zeros((n, 2 * CMP_HIDDEN), F32) for _ in range(NSA_GROUPS)]
    for j in range(CMP_STRIDE):
        xj = p_ref[0, :, j * width:(j + 1) * width].astype(BF16)
        for g in range(NSA_GROUPS):
            acc[g] = acc[g] + _dot(xj[:, g * NSA_DH:(g + 1) * NSA_DH], wj_ref[j])
    for g in range(NSA_GROUPS):
        first, second = acc[g][:, :CMP_HIDDEN], acc[g][:, CMP_HIDDEN:]
        hid = first + pltpu.roll(second, n - 1, axis=0) + bias
        o_ref[0, g] = _dot(_silu(hid).astype(BF16), w2)


def _compress(t, pe, w1, w2):
    B, S, width = t.shape
    n = S // CMP_STRIDE
    pieces = t.reshape(B, n, CMP_STRIDE * width)
    pe_flat = jnp.pad(pe.reshape(1, CMP_BLOCK * NSA_DH), ((0, 7), (0, 0)))
    w1b = w1.astype(BF16)
    w1r = w1b.reshape(2, CMP_STRIDE, NSA_DH, CMP_HIDDEN)
    wj = jnp.concatenate([w1r[0], w1r[1]], axis=-1)
    full = lambda a: pl.BlockSpec(a.shape, lambda b: (0,) * a.ndim)
    args = (pieces, pe_flat, w1b, wj, w2.astype(BF16))
    return pl.pallas_call(
        _compress_kernel,
        grid=(B,),
        in_specs=[pl.BlockSpec((1, n, CMP_STRIDE * width), lambda b: (b, 0, 0))] + [full(a) for a in args[1:]],
        out_specs=pl.BlockSpec((1, NSA_GROUPS, n, NSA_DH), lambda b: (b, 0, 0, 0)),
        out_shape=jax.ShapeDtypeStruct((B, NSA_GROUPS, n, NSA_DH), F32),
        compiler_params=_params("parallel"),
        name="nsa_compress",
    )(*args)


def _nsa_attn_kernel(q_ref, kcm_ref, vct_ref, kse_ref, vst_ref, kwn_ref, vwt_ref, gt_ref, ovt_ref,
                     o_ref, m_ref, acc_ref, s0_ref, s1_ref, ms_ref, sw0_ref, sw1_ref, sw2_ref, mw_ref,
                     *, n_cmp, n_blk):
    tq = ATT_TILE
    qi = pl.program_id(2)
    t0 = qi * tq
    t_row = t0 + lax.broadcasted_iota(jnp.int32, (1, tq), 1)
    q_t = jnp.transpose(q_ref[0].astype(F32)).astype(BF16)
    q_heads = [q_t[h * NSA_DH:(h + 1) * NSA_DH] for h in range(NSA_HPG)]

    def scores(buf, k_tile, ws, bias):
        s_ref, mt_ref = buf
        for h, w in enumerate(ws):
            s = _dot(k_tile, w)
            if bias is not None:
                s = s + bias
            s_ref[h] = s
            mt_ref[h] = jnp.max(s, axis=0, keepdims=True)

    def absorb(br, buf, vt_tile):
        s_ref, mt_ref = buf
        for h in range(NSA_HPG):
            m_old = m_ref[br, h]
            m_new = jnp.maximum(m_old, mt_ref[h])
            p = jnp.exp2(s_ref[h] - m_new).astype(BF16)
            acc_ref[br, h] = jnp.exp2(m_old - m_new) * acc_ref[br, h] + _dot(vt_tile, p)
            m_ref[br, h] = m_new

    def result(br, h):
        acc = acc_ref[br, h]
        return acc[0:NSA_DH] / acc[NSA_DH:NSA_DH + 1]

    m_ref[...] = jnp.full(m_ref.shape, NEG, F32)
    acc_ref[...] = jnp.zeros(acc_ref.shape, F32)
    r_idx = lax.broadcasted_iota(jnp.int32, (tq, tq), 0)
    c_full = lax.broadcasted_iota(jnp.int32, (tq, tq), 1)
    diag_bias = jnp.where(r_idx <= c_full, 0.0, NEG)
    far_bias = jnp.where((r_idx > c_full) & (qi >= 2), 0.0, NEG)
    mid_bias = jnp.where(jnp.broadcast_to(qi >= 1, (tq, tq)), 0.0, NEG)

    kcm = kcm_ref[0, 0]
    s_cmp = [_dot(kcm, q_heads[h]) for h in range(NSA_HPG)]
    w_bufs = ((sw0_ref, mw_ref.at[0]), (sw1_ref, mw_ref.at[1]), (sw2_ref, mw_ref.at[2]))
    w_tiles = (jnp.maximum(qi - 2, 0), jnp.maximum(qi - 1, 0), qi)
    no_bias_rows = jnp.zeros((LANES - NSA_DH, tq), BF16)
    w_win = [jnp.concatenate([q_heads[h], no_bias_rows], axis=0) for h in range(NSA_HPG)]
    for buf, kt, bias in zip(w_bufs, w_tiles, (far_bias, mid_bias, diag_bias)):
        scores(buf, kwn_ref[0, 0, kt], w_win, bias)

    c_idx = lax.broadcasted_iota(jnp.int32, (n_cmp, 1), 0)
    cvalid = (c_idx * CMP_STRIDE + (CMP_BLOCK - 1)) <= t_row
    anyvalid = jnp.where(t_row >= CMP_BLOCK - 1, 1.0, 0.0)
    vct = vct_ref[0, 0]
    psum = jnp.zeros((n_cmp, tq), F32)
    o_cmp = []
    for h in range(NSA_HPG):
        s = jnp.where(cvalid, s_cmp[h], NEG)
        e = jnp.exp2(s - jnp.max(s, axis=0, keepdims=True))
        p = e * (anyvalid / jnp.sum(e, axis=0, keepdims=True))
        psum = psum + p
        o_cmp.append(_dot(vct, p.astype(BF16)))
    p_hi = psum.astype(BF16)
    p_lo = (psum - p_hi.astype(F32)).astype(BF16)
    ovt = ovt_ref[...]
    imp = _dot(ovt, p_hi) + _dot(ovt, p_lo)

    j_full = lax.broadcasted_iota(jnp.int32, (n_blk, tq), 0)
    cur = t_row // SEL_BLOCK
    forced = (j_full == 0) | (j_full == cur) | (j_full == cur - 1)
    causal_blk = j_full * SEL_BLOCK <= t_row
    val = jnp.where(forced, BIG, jnp.where(causal_blk, imp, NEG))
    sub = 8
    groups = [val[r:r + sub] for r in range(0, n_blk, sub)]
    j_grp = lax.broadcasted_iota(jnp.int32, (sub, tq), 0)
    ranks = [jnp.zeros((sub, tq), F32) for _ in groups]
    for k in range(n_blk):
        vk = val[k:k + 1, :]
        for gi, vg in enumerate(groups):
            lo = gi * sub
            ge = jnp.where(vk >= vg, 1.0, 0.0)
            gt = jnp.where(vk > vg, 1.0, 0.0)
            if lo > k:
                inc = ge
            elif lo + sub - 1 <= k:
                inc = gt
            else:
                inc = jnp.where(j_grp + lo > k, ge, gt)
            ranks[gi] = ranks[gi] + inc
    rank = jnp.concatenate(ranks, axis=0)
    sel_bias = jnp.where(rank < float(SEL_TOP), 0.0, NEG).astype(BF16)

    pad = [jnp.zeros((LANES - NSA_DH - n_blk, tq), BF16)] if n_blk < LANES - NSA_DH else []
    w_sel = [jnp.concatenate([q_heads[h], sel_bias] + pad, axis=0) for h in range(NSA_HPG)]
    buf0 = (s0_ref, ms_ref.at[0])
    buf1 = (s1_ref, ms_ref.at[1])
    SEL, WIN = 0, 1
    scores(buf0, kse_ref[0, 0, qi], w_sel, diag_bias)

    for buf, kt in zip(w_bufs, w_tiles):
        absorb(WIN, buf, vwt_ref[0, 0, kt])

    def seq_tile(j):
        return jnp.where(j == 0, qi, j - 1)

    def pair(i, carry):
        scores(buf1, kse_ref[0, 0, 2 * i], w_sel, None)
        absorb(SEL, buf0, vst_ref[0, 0, seq_tile(2 * i)])
        scores(buf0, kse_ref[0, 0, 2 * i + 1], w_sel, None)
        absorb(SEL, buf1, vst_ref[0, 0, 2 * i])
        return carry

    lax.fori_loop(0, qi // 2, pair, 0)

    @pl.when(qi % 2 == 0)
    def _():
        absorb(SEL, buf0, vst_ref[0, 0, seq_tile(qi)])

    @pl.when(qi % 2 == 1)
    def _():
        scores(buf1, kse_ref[0, 0, qi - 1], w_sel, None)
        absorb(SEL, buf0, vst_ref[0, 0, seq_tile(qi - 1)])
        absorb(SEL, buf1, vst_ref[0, 0, qi - 1])

    gates = gt_ref[0, 0]
    outs = []
    for h in range(NSA_HPG):
        outs.append(gates[3 * h:3 * h + 1] * o_cmp[h] + gates[3 * h + 1:3 * h + 2] * result(SEL, h)
                    + gates[3 * h + 2:3 * h + 3] * result(WIN, h))
    o_ref[0] = jnp.transpose(jnp.concatenate(outs, axis=0)).astype(BF16)


def _nsa_attn(q, kcmp, vcmp, kse, vst, kwn, vwt, gt):
    B, S, _ = q.shape
    G, DH, tq = NSA_GROUPS, NSA_DH, ATT_TILE
    assert WINDOW == 2 * tq
    n_cmp = kcmp.shape[2]
    nt = S // tq
    ns = S // SEL_BLOCK
    assert ns <= LANES - DH
    kcm = kcmp.astype(BF16)
    vct = vcmp.transpose(0, 1, 3, 2).astype(BF16)
    cs = jnp.arange(n_cmp)[None, :] * CMP_STRIDE
    ss = jnp.arange(ns)[:, None] * SEL_BLOCK
    n_valid = (S - CMP_BLOCK) // CMP_STRIDE + 1
    ovt = ((cs < ss + SEL_BLOCK) & (cs + CMP_BLOCK > ss) & (jnp.arange(n_cmp)[None, :] < n_valid)).astype(BF16)

    per_bg = lambda a: pl.BlockSpec((1, 1) + a.shape[2:], lambda b, g, i: (b, g) + (0,) * (a.ndim - 2))
    return pl.pallas_call(
        functools.partial(_nsa_attn_kernel, n_cmp=n_cmp, n_blk=ns),
        grid=(B, G, nt),
        in_specs=[pl.BlockSpec((1, tq, NSA_HPG * DH), lambda b, g, i: (b, i, g)),
                  per_bg(kcm), per_bg(vct), per_bg(kse), per_bg(vst), per_bg(kwn), per_bg(vwt),
                  pl.BlockSpec((1, 1, 16, tq), lambda b, g, i: (b, g, 0, i)),
                  pl.BlockSpec(ovt.shape, lambda b, g, i: (0, 0))],
        out_specs=pl.BlockSpec((1, tq, NSA_HPG * DH), lambda b, g, i: (b, i, g)),
        out_shape=jax.ShapeDtypeStruct((B, S, NSA_Q), BF16),
        scratch_shapes=[pltpu.VMEM((2, NSA_HPG, 1, tq), F32), pltpu.VMEM((2, NSA_HPG, V_ROWS, tq), F32)]
        + [pltpu.VMEM((NSA_HPG, tq, tq), F32)] * 2 + [pltpu.VMEM((2, NSA_HPG, 1, tq), F32)]
        + [pltpu.VMEM((NSA_HPG, tq, tq), F32)] * 3 + [pltpu.VMEM((3, NSA_HPG, 1, tq), F32)],
        compiler_params=_params("parallel", "parallel", "arbitrary"),
        name="nsa_attn",
    )(q, kcm, vct, kse, vst, kwn, vwt, gt, ovt)


def _nsa_out_kernel(og_ref, zs_ref, x_ref, pn_ref, wo_ref, o_ref):
    u = (og_ref[...].astype(F32) * zs_ref[...].astype(F32)).astype(BF16)
    o_ref[...] = x_ref[...] + _rms(_dot(u, wo_ref[...]), pn_ref[...])


def _nsa_out(og, zs, x2, post_g, w_out):
    T = x2.shape[0]
    row = lambda w: pl.BlockSpec((ROW_TILE, w), lambda i: (i, 0))
    full = lambda a: pl.BlockSpec(a.shape, lambda i: (0,) * a.ndim)
    args = (og, zs, x2, post_g.reshape(1, D_MODEL), w_out.astype(BF16))
    return pl.pallas_call(
        _nsa_out_kernel,
        grid=(T // ROW_TILE,),
        in_specs=[row(NSA_Q), row(NSA_Q), row(D_MODEL)] + [full(a) for a in args[3:]],
        out_specs=row(D_MODEL),
        out_shape=jax.ShapeDtypeStruct((T, D_MODEL), F32),
        compiler_params=_params("parallel"),
        name="nsa_out",
    )(*args)


def _gla_layer(x, pre_g, post_g, w_in, w_gk_up, b_gk, head_norm, w_out):
    B, S, D = x.shape
    q, k, v, gk, zs = _gla_proj(x.reshape(B * S, D), pre_g, w_in, w_gk_up, b_gk)
    r3 = lambda a: a.reshape(B, S, a.shape[-1])
    return _gla_core(r3(q), r3(k), r3(v), r3(gk), r3(zs), x, head_norm, post_g, w_out)


def _nsa_layer(x, positions, pre_g, post_g, w_in, b_gate, pe_k, pe_v, ck_w1, ck_w2, cv_w1, cv_w2, w_out):
    B, S, D = x.shape
    q, kc, vc, kse, kwn, vst, vwt, gt, zs = _nsa_proj(x, positions, pre_g, w_in, b_gate)
    kcmp = _compress(kc, pe_k, ck_w1, ck_w2)
    vcmp = _compress(vc, pe_v, cv_w1, cv_w2)
    og = _nsa_attn(q, kcmp, vcmp, kse, vst, kwn, vwt, gt)
    out = _nsa_out(og.reshape(B * S, NSA_Q), zs.reshape(B * S, NSA_Q), x.reshape(B * S, D), post_g, w_out)
    return out.reshape(B, S, D)


def kernel(x, positions, pre_norm, post_norm, gla_w_in, gla_w_gk_up, gla_b_gk, gla_head_norm, gla_w_out,
           nsa_w_in, nsa_b_gate, nsa_pe_k, nsa_pe_v, nsa_ck_w1, nsa_ck_w2, nsa_cv_w1, nsa_cv_w2, nsa_w_out):
    depth = pre_norm.shape[0]
    for i in range(depth):
        j = i // 2
        if i % 2 == 0:
            x = _gla_layer(x, pre_norm[i], post_norm[i], gla_w_in[j], gla_w_gk_up[j], gla_b_gk[j],
                           gla_head_norm[j], gla_w_out[j])
        else:
            x = _nsa_layer(x, positions, pre_norm[i], post_norm[i], nsa_w_in[j], nsa_b_gate[j],
                           nsa_pe_k[j], nsa_pe_v[j], nsa_ck_w1[j], nsa_ck_w2[j], nsa_cv_w1[j],
                           nsa_cv_w2[j], nsa_w_out[j])
    return x
```
